```python
import jax, jax.numpy as jnp
from jax import lax
import numpy as np

D_MODEL = 1024
BATCH = 16
SEQ = 4096
DEPTH = 1

PLE_DIM = 256
POOL_GROUPS = 4
POOL_WINDOWS = (2, 4, 8, 16)
POOL_CH = D_MODEL // 8
POOL_W = POOL_GROUPS * POOL_CH
N_HEADS = 8
HEAD_DIM = 64
ATT_W = N_HEADS * HEAD_DIM
Q_BLOCK = 128
N_BRANCH = 2
IN_W = POOL_W + 3 * ATT_W + N_BRANCH * D_MODEL
N_EXPERTS = 32
TOP_K = 4
D_EXPERT = D_MODEL
SWIGLU_LIMIT = 7.0
SWIGLU_ALPHA = 1.702
EXPERT_BLOCK = 512
EPS = 1e-6

kernel_name = "hybrid_pool_stickbreak_moe_ple"


def rmsnorm(x, g):
    xf = x.astype(jnp.float32)
    y = xf * lax.rsqrt(jnp.mean(xf * xf, axis=-1, keepdims=True) + EPS)
    return (y * g.astype(jnp.float32)).astype(x.dtype)


def causal_multiscale_pool(u):
    s = u.shape[1]
    uf = u.astype(jnp.float32)
    csum = jnp.cumsum(uf, axis=1)
    t = jnp.arange(s)
    outs = []
    for g, w in enumerate(POOL_WINDOWS):
        c = csum[:, :, g]
        lag = jnp.pad(c, ((0, 0), (w, 0), (0, 0)))[:, :s]
        count = jnp.minimum(t + 1, w).astype(jnp.float32)[None, :, None]
        outs.append((c - lag) / count - uf[:, :, g])
    return jnp.stack(outs, axis=2).astype(u.dtype)


def stick_breaking_attention(q, k, v):
    s = q.shape[2]
    scale = HEAD_DIM ** -0.5
    outs = []
    for blk in range(s // Q_BLOCK):
        q0 = blk * Q_BLOCK
        kv_len = q0 + Q_BLOCK
        qb = q[:, :, q0:kv_len]
        kb = k[:, :, :kv_len]
        vb = v[:, :, :kv_len]
        z = jnp.einsum('bhqd,bhkd->bhqk', qb, kb).astype(jnp.float32) * scale
        t_pos = q0 + jnp.arange(Q_BLOCK)
        s_pos = jnp.arange(kv_len)
        mask = s_pos[None, :] < t_pos[:, None]
        log_keep = jnp.where(mask, -jax.nn.softplus(z), 0.0)
        later = lax.cumsum(log_keep, axis=3, reverse=True) - log_keep
        weights = jnp.where(mask, jnp.exp(jax.nn.log_sigmoid(z) + later), 0.0)
        outs.append(jnp.einsum('bhqk,bhkd->bhqd', weights.astype(v.dtype), vb))
    return jnp.concatenate(outs, axis=2)


def hybrid_mixer(h, w_in, w_pool_grp, pool_scale, w_pool_up, w_attn_up, w_out):
    b, s, _ = h.shape
    proj = h @ w_in
    u, q, k, v, gate_logits = jnp.split(
        proj, [POOL_W, POOL_W + ATT_W, POOL_W + 2 * ATT_W, POOL_W + 3 * ATT_W], axis=-1)
    pooled = causal_multiscale_pool(u.reshape(b, s, POOL_GROUPS, POOL_CH))
    y_pool = jnp.einsum('bsgc,gcd->bsgd', pooled, w_pool_grp).reshape(b, s, POOL_W) * pool_scale
    to_heads = lambda t: t.reshape(b, s, N_HEADS, HEAD_DIM).transpose(0, 2, 1, 3)
    o = stick_breaking_attention(to_heads(q), to_heads(k), to_heads(v))
    y_att = o.transpose(0, 2, 1, 3).reshape(b, s, ATT_W)
    g_pool, g_att = jnp.split(jax.nn.sigmoid(gate_logits), 2, axis=-1)
    merged = g_pool * (y_pool @ w_pool_up) + g_att * (y_att @ w_attn_up)
    return merged @ w_out


def moe_ffn(h, w_router, b_router, w_gu, b_gu, w_down, b_down):
    b, s, d = h.shape
    n_tok = b * s
    n_assign = n_tok * TOP_K
    xt = h.reshape(n_tok, d)
    logits = (xt @ w_router + b_router).astype(jnp.float32)
    top_logit, top_e = lax.top_k(logits, TOP_K)
    top_w = jax.nn.softmax(top_logit, axis=-1)
    flat_e = top_e.reshape(-1)
    order = jnp.argsort(flat_e)
    sorted_e = flat_e[order]
    token_of = order // TOP_K
    counts = jax.ops.segment_sum(jnp.ones_like(flat_e), flat_e, num_segments=N_EXPERTS)
    starts = jnp.cumsum(counts) - counts
    padded = (counts + EXPERT_BLOCK - 1) // EXPERT_BLOCK * EXPERT_BLOCK
    pad_ends = jnp.cumsum(padded)
    pad_starts = pad_ends - padded
    dest = pad_starts[sorted_e] + jnp.arange(n_assign) - starts[sorted_e]
    n_blocks = -(-n_assign // EXPERT_BLOCK) + N_EXPERTS
    n_pad = n_blocks * EXPERT_BLOCK
    buf = jnp.zeros((n_pad, d), h.dtype).at[dest].set(xt[token_of])
    block_e = jnp.minimum(
        jnp.searchsorted(pad_ends, jnp.arange(n_blocks) * EXPERT_BLOCK, side='right'),
        N_EXPERTS - 1)

    def expert_block(args):
        xb, e = args
        gu = xb @ w_gu[e] + b_gu[e]
        gate = jnp.minimum(gu[:, :D_EXPERT], SWIGLU_LIMIT)
        up = jnp.clip(gu[:, D_EXPERT:], -SWIGLU_LIMIT, SWIGLU_LIMIT)
        glu = gate * jax.nn.sigmoid(SWIGLU_ALPHA * gate)
        return ((up + 1.0) * glu) @ w_down[e] + b_down[e]

    y_buf = lax.map(expert_block, (buf.reshape(n_blocks, EXPERT_BLOCK, d), block_e)).reshape(n_pad, d)
    contrib = y_buf[dest].astype(jnp.float32) * top_w.reshape(-1)[order][:, None]
    out = jax.ops.segment_sum(contrib, token_of, num_segments=n_tok)
    return out.astype(h.dtype).reshape(b, s, d)


def setup_inputs(seed: int = 0) -> dict:
    key = jax.random.key(seed)
    ks = jax.random.split(key, 24)
    f32 = jnp.float32

    def nrm(k, shape, fan_in):
        return jax.random.normal(k, shape, f32) * (fan_in ** -0.5)

    def gain(k, shape):
        return 1.0 + 0.02 * jax.random.normal(k, shape, f32)

    def bias(k, shape, sc=0.02):
        return sc * jax.random.normal(k, shape, f32)

    return {
        "x": jax.random.normal(ks[0], (BATCH, SEQ, D_MODEL), f32),
        "p": jax.random.normal(ks[1], (DEPTH, BATCH, SEQ, PLE_DIM), f32),
        "g_mix": gain(ks[2], (DEPTH, D_MODEL)),
        "w_in": nrm(ks[3], (DEPTH, D_MODEL, IN_W), D_MODEL),
        "w_pool_grp": nrm(ks[4], (DEPTH, POOL_GROUPS, POOL_CH, POOL_CH), POOL_CH),
        "pool_scale": gain(ks[5], (DEPTH, POOL_W)),
        "w_pool_up": nrm(ks[6], (DEPTH, POOL_W, D_MODEL), POOL_W),
        "w_attn_up": nrm(ks[7], (DEPTH, ATT_W, D_MODEL), ATT_W),
        "w_out": nrm(ks[8], (DEPTH, D_MODEL, D_MODEL), D_MODEL),
        "g_ffn": gain(ks[9], (DEPTH, D_MODEL)),
        "w_router": nrm(ks[10], (DEPTH, D_MODEL, N_EXPERTS), D_MODEL),
        "b_router": bias(ks[11], (DEPTH, N_EXPERTS), 0.01),
        "w_gu": nrm(ks[12], (DEPTH, N_EXPERTS, D_MODEL, 2 * D_EXPERT), D_MODEL),
        "b_gu": bias(ks[13], (DEPTH, N_EXPERTS, 2 * D_EXPERT)),
        "w_down": nrm(ks[14], (DEPTH, N_EXPERTS, D_EXPERT, D_MODEL), D_EXPERT),
        "b_down": bias(ks[15], (DEPTH, N_EXPERTS, D_MODEL)),
        "g_ple": gain(ks[16], (DEPTH, D_MODEL)),
        "w_ple_gate": nrm(ks[17], (DEPTH, D_MODEL, D_MODEL), D_MODEL),
        "w_ple_proj": nrm(ks[18], (DEPTH, PLE_DIM, D_MODEL), PLE_DIM),
        "g_final": gain(ks[19], (D_MODEL,)),
    }


def reference(x, p, g_mix, w_in, w_pool_grp, pool_scale, w_pool_up, w_attn_up, w_out,
              g_ffn, w_router, b_router, w_gu, b_gu, w_down, b_down,
              g_ple, w_ple_gate, w_ple_proj, g_final):
    for i in range(DEPTH):
        h = rmsnorm(x, g_mix[i])
        x = x + hybrid_mixer(h, w_in[i], w_pool_grp[i], pool_scale[i],
                             w_pool_up[i], w_attn_up[i], w_out[i])
        h = rmsnorm(x, g_ffn[i])
        x = x + moe_ffn(h, w_router[i], b_router[i], w_gu[i], b_gu[i], w_down[i], b_down[i])
        hp = rmsnorm(x, g_ple[i])
        x = x + jax.nn.sigmoid(hp @ w_ple_gate[i]) * (p[i] @ w_ple_proj[i])
    return rmsnorm(x, g_final)
```

```python
import functools

import jax
import jax.numpy as jnp
from jax import lax
from jax.experimental import pallas as pl
from jax.experimental.pallas import tpu as pltpu

F32 = jnp.float32
BF16 = jnp.bfloat16

N_HEADS = 8
HEAD_DIM = 64
HEADS_PER_STEP = 2
POOL_WINDOWS = (2, 4, 8, 16)
POOL_CH = 128
POOL_HIST = 16
N_EXPERTS = 32
TOP_K = 4
EXPERT_BLOCK = 512
SWIGLU_LIMIT = 7.0
SWIGLU_ALPHA = 1.702
EPS = 1e-6
LANES = 128
VMEM_LIMIT = 56 * 1024 * 1024


def _params(n_axes):
    return pltpu.CompilerParams(dimension_semantics=("arbitrary",) * n_axes, vmem_limit_bytes=VMEM_LIMIT)


def _rms(xf, g):
    return xf * lax.rsqrt(jnp.mean(xf * xf, axis=-1, keepdims=True) + EPS) * g


def _dot(a, b):
    return jnp.dot(a, b, preferred_element_type=F32)


def _split_bf16(a):
    hi = a.astype(BF16)
    lo = (a - hi.astype(F32)).astype(BF16)
    return hi, lo


def _mix_in_kernel(x_ref, g_ref, wu_ref, wqkv_ref, wgp_ref, wga_ref, wgrp_ref, pscale_ref, wpu_ref,
                   q_ref, k_ref, v_ref, pb_ref, ga_ref, hist_ref):
    s = pl.program_id(1)
    ts = x_ref.shape[1]
    att_w = q_ref.shape[2]

    @pl.when(s == 0)
    def _():
        hist_ref[...] = jnp.zeros_like(hist_ref)

    h = _rms(x_ref[0], g_ref[...]).astype(BF16)
    u = _dot(h, wu_ref[...])
    ext = jnp.concatenate([hist_ref[...], u], axis=0)
    hist_ref[...] = u[ts - POOL_HIST:, :]
    t_pos = s * ts + lax.broadcasted_iota(jnp.int32, (ts, 1), 0)
    y_groups = []
    for g, w in enumerate(POOL_WINDOWS):
        a = ext[:, g * POOL_CH:(g + 1) * POOL_CH]
        sh = 1
        while sh < w:
            a = a + pltpu.roll(a, sh, axis=0)
            sh *= 2
        count = jnp.minimum(t_pos + 1, w).astype(F32)
        pooled = a[POOL_HIST:, :] / count - u[:, g * POOL_CH:(g + 1) * POOL_CH]
        y_groups.append(_dot(pooled.astype(BF16), wgrp_ref[g]))
    y_pool = jnp.concatenate(y_groups, axis=1) * pscale_ref[...]
    pool_up = _dot(y_pool.astype(BF16), wpu_ref[...])
    g_pool = jax.nn.sigmoid(_dot(h, wgp_ref[...]))
    pb_ref[0] = (g_pool * pool_up).astype(BF16)
    ga_ref[0] = jax.nn.sigmoid(_dot(h, wga_ref[...])).astype(BF16)
    qkv = _dot(h, wqkv_ref[...])
    q_ref[0] = (qkv[:, :att_w] * (HEAD_DIM ** -0.5)).astype(BF16)
    k_ref[0] = qkv[:, att_w:2 * att_w].astype(BF16)
    v_ref[0] = qkv[:, 2 * att_w:].astype(BF16)


def _mix_in(x, g_mix, w_u, w_qkv, w_gp, w_ga, w_grp, pool_scale, w_pool_up, ts):
    b, s, d = x.shape
    pool_w = w_u.shape[1]
    att_w = w_qkv.shape[1] // 3
    const = lambda *shape: pl.BlockSpec(shape, lambda i, j: (0,) * len(shape))
    tile = lambda w: pl.BlockSpec((1, ts, w), lambda i, j: (i, j, 0))
    return pl.pallas_call(
        _mix_in_kernel,
        grid=(b, s // ts),
        in_specs=[tile(d), const(1, d), const(d, pool_w), const(d, 3 * att_w), const(d, d), const(d, d),
                  const(len(POOL_WINDOWS), POOL_CH, POOL_CH), const(1, pool_w), const(pool_w, d)],
        out_specs=[tile(att_w), tile(att_w), tile(att_w), tile(d), tile(d)],
        out_shape=[jax.ShapeDtypeStruct((b, s, att_w), BF16)] * 3 + [jax.ShapeDtypeStruct((b, s, d), BF16)] * 2,
        scratch_shapes=[pltpu.VMEM((POOL_HIST, pool_w), F32)],
        compiler_params=_params(2),
        name="mix_in",
    )(x, g_mix, w_u, w_qkv, w_gp, w_ga, w_grp, pool_scale, w_pool_up)


def _attn_kernel(q_ref, k_ref, v_ref, tri_ref, o_ref, *, blk):
    i = pl.program_id(2)
    q2 = q_ref[0]
    lane = lax.broadcasted_iota(jnp.int32, (blk, LANES), 1)
    row = lax.broadcasted_iota(jnp.int32, (blk, blk), 0)
    col = lax.broadcasted_iota(jnp.int32, (blk, blk), 1)
    causal = col < row

    def head(hm):
        qh = jnp.where(hm, q2, jnp.zeros_like(q2))

        def block(j, carry, diagonal):
            acc, run = carry
            start = pl.multiple_of(j * blk, blk)
            kj = k_ref[0, pl.ds(start, blk), :]
            vj = v_ref[0, pl.ds(start, blk), :]
            z = lax.dot_general(qh, kj, (((1,), (1,)), ((), ())), preferred_element_type=F32)
            log_keep = -(jnp.maximum(z, 0.0) + jnp.log(1.0 + jnp.exp(-jnp.abs(z))))
            if diagonal:
                log_keep = jnp.where(causal, log_keep, 0.0)
            hi, lo = _split_bf16(log_keep)
            incl = _dot(jnp.concatenate([hi, lo], axis=1), tri_ref[...])
            later = incl - log_keep + run
            w = jnp.exp(z + log_keep + later)
            if diagonal:
                w = jnp.where(causal, w, 0.0)
            acc = acc + _dot(w.astype(BF16), vj)
            return acc, run + incl[:, 0:1]

        carry = (jnp.zeros((blk, LANES), F32), jnp.zeros((blk, 1), F32))
        carry = block(i, carry, True)
        carry = lax.fori_loop(0, i, lambda n, c: block(i - 1 - n, c, False), carry)
        return carry[0]

    o_ref[0] = jnp.where(lane < HEAD_DIM, head(lane < HEAD_DIM), head(lane >= HEAD_DIM)).astype(o_ref.dtype)


def _attention(q, k, v, blk):
    b, s, att_w = q.shape
    tri = (jnp.arange(blk)[:, None] >= jnp.arange(blk)[None, :]).astype(BF16)
    tri2 = jnp.concatenate([tri, tri], axis=0)
    qo_spec = pl.BlockSpec((1, blk, LANES), lambda bi, hp, i: (bi, i, hp))
    kv_spec = pl.BlockSpec((1, s, LANES), lambda bi, hp, i: (bi, 0, hp))
    return pl.pallas_call(
        functools.partial(_attn_kernel, blk=blk),
        grid=(b, att_w // LANES, s // blk),
        in_specs=[qo_spec, kv_spec, kv_spec, pl.BlockSpec((2 * blk, blk), lambda bi, hp, i: (0, 0))],
        out_specs=qo_spec,
        out_shape=jax.ShapeDtypeStruct((b, s, att_w), BF16),
        compiler_params=_params(3),
        name="attn",
    )(q, k, v, tri2)


def _mix_out_kernel(x_ref, ya_ref, pb_ref, ga_ref, wau_ref, wo_ref, gf_ref, wrh_ref, wrl_ref, br_ref, lt_ref,
                    x1_ref, h2_ref, e_ref, r_ref, w_ref, cnt_ref):
    t = x_ref.shape[0]

    @pl.when(pl.program_id(0) == 0)
    def _():
        cnt_ref[...] = jnp.zeros_like(cnt_ref)

    att_up = _dot(ya_ref[...], wau_ref[...])
    merged = pb_ref[...].astype(F32) + ga_ref[...].astype(F32) * att_up
    x1 = x_ref[...] + _dot(merged.astype(BF16), wo_ref[...])
    x1_ref[...] = x1
    h2 = _rms(x1, gf_ref[...])
    h2_ref[...] = h2

    hi, lo = _split_bf16(h2)
    logits = _dot(hi, wrh_ref[...]) + _dot(lo, wrh_ref[...]) + _dot(hi, wrl_ref[...]) + br_ref[...]
    lane_e = lax.broadcasted_iota(jnp.int32, (t, N_EXPERTS), 1)
    vals, idxs = [], []
    for _ in range(TOP_K):
        m = jnp.max(logits, axis=-1, keepdims=True)
        idx = jnp.min(jnp.where(logits == m, lane_e, N_EXPERTS), axis=-1, keepdims=True)
        vals.append(m)
        idxs.append(idx)
        logits = jnp.where(lane_e == idx, -jnp.inf, logits)
    exps = [jnp.exp(m - vals[0]) for m in vals]
    denom = exps[0] + exps[1] + exps[2] + exps[3]

    onehot = jnp.zeros((t, N_EXPERTS), F32)
    for idx in idxs:
        onehot = onehot + jnp.where(lane_e == idx, 1.0, 0.0)
    pos = cnt_ref[...] + _dot(lt_ref[...], onehot.astype(BF16))
    cnt_ref[...] = cnt_ref[...] + jnp.sum(onehot, axis=0, keepdims=True)

    lane = lax.broadcasted_iota(jnp.int32, (t, LANES), 1)
    e_out = jnp.zeros((t, LANES), jnp.int32)
    r_out = jnp.zeros((t, LANES), jnp.int32)
    w_out = jnp.zeros((t, LANES), F32)
    for kk in range(TOP_K):
        rank = jnp.sum(jnp.where(lane_e == idxs[kk], pos, 0.0), axis=-1, keepdims=True).astype(jnp.int32)
        e_out = jnp.where(lane == kk, idxs[kk], e_out)
        r_out = jnp.where(lane == kk, rank, r_out)
        w_out = jnp.where(lane == kk, exps[kk] / denom, w_out)
    e_ref[...] = e_out
    r_ref[...] = r_out
    w_ref[...] = w_out


def _mix_out(x2d, y_att, pool_branch, g_att, w_attn_up, w_out, g_ffn, wr_hi, wr_lo, b_router, t):
    n, d = x2d.shape
    att_w = y_att.shape[1]
    lower = (jnp.arange(t)[:, None] > jnp.arange(t)[None, :]).astype(BF16)
    const = lambda *shape: pl.BlockSpec(shape, lambda i: (0,) * len(shape))
    tile = lambda w: pl.BlockSpec((t, w), lambda i: (i, 0))
    return pl.pallas_call(
        _mix_out_kernel,
        grid=(n // t,),
        in_specs=[tile(d), tile(att_w), tile(d), tile(d), const(att_w, d), const(d, d), const(1, d),
                  const(d, N_EXPERTS), const(d, N_EXPERTS), const(1, N_EXPERTS), const(t, t)],
        out_specs=[tile(d), tile(d), tile(LANES), tile(LANES), tile(LANES), const(1, N_EXPERTS)],
        out_shape=[jax.ShapeDtypeStruct((n, d), F32), jax.ShapeDtypeStruct((n, d), F32),
                   jax.ShapeDtypeStruct((n, LANES), jnp.int32), jax.ShapeDtypeStruct((n, LANES), jnp.int32),
                   jax.ShapeDtypeStruct((n, LANES), F32), jax.ShapeDtypeStruct((1, N_EXPERTS), F32)],
        compiler_params=_params(1),
        name="mix_out",
    )(x2d, y_att, pool_branch, g_att, w_attn_up, w_out, g_ffn, wr_hi, wr_lo, b_router, lower)


def _row_copy(src_ref, src_row, dst_ref, dst_row, sem):
    return pltpu.make_async_copy(src_ref.at[pl.ds(src_row, 1)], dst_ref.at[pl.ds(dst_row, 1)], sem)


def _dispatch_kernel(pe_ref, cnt_ref, dest_ref, h_ref, buf_ref, zero_ref, sem, zsem):
    t = h_ref.shape[0]

    @pl.when(pl.program_id(0) == 0)
    def _():
        zero_ref[...] = jnp.zeros_like(zero_ref)

        def zero_copy(e):
            start = pl.multiple_of(pe_ref[e] - EXPERT_BLOCK, EXPERT_BLOCK)
            return pltpu.make_async_copy(zero_ref, buf_ref.at[pl.ds(start, EXPERT_BLOCK)], zsem)

        def start(e, c):
            @pl.when(cnt_ref[e] > 0)
            def _():
                zero_copy(e).start()
            return c

        def wait(e, c):
            @pl.when(cnt_ref[e] > 0)
            def _():
                zero_copy(e).wait()
            return c

        lax.fori_loop(0, N_EXPERTS, start, 0)
        lax.fori_loop(0, N_EXPERTS, wait, 0)

    def start_rows(r, c):
        for kk in range(TOP_K):
            _row_copy(h_ref, r, buf_ref, dest_ref[0, 0, TOP_K * r + kk], sem).start()
        return c

    def wait_rows(r, c):
        for kk in range(TOP_K):
            _row_copy(h_ref, r, buf_ref, dest_ref[0, 0, TOP_K * r + kk], sem).wait()
        return c

    lax.fori_loop(0, t, start_rows, 0)
    lax.fori_loop(0, t, wait_rows, 0)


def _dispatch(h2, dest3, pad_ends, counts, n_pad, t):
    n, d = h2.shape
    return pl.pallas_call(
        _dispatch_kernel,
        grid_spec=pltpu.PrefetchScalarGridSpec(
            num_scalar_prefetch=2,
            grid=(n // t,),
            in_specs=[pl.BlockSpec((1, 1, TOP_K * t), lambda i, pe, cn: (i, 0, 0), memory_space=pltpu.SMEM),
                      pl.BlockSpec((t, d), lambda i, pe, cn: (i, 0))],
            out_specs=pl.BlockSpec(memory_space=pl.ANY),
            scratch_shapes=[pltpu.VMEM((EXPERT_BLOCK, d), h2.dtype), pltpu.SemaphoreType.DMA,
                            pltpu.SemaphoreType.DMA],
        ),
        out_shape=jax.ShapeDtypeStruct((n_pad, d), h2.dtype),
        compiler_params=_params(1),
        name="dispatch",
    )(pad_ends, counts, dest3, h2)


def _expert_kernel(be_ref, nu_ref, x_ref, wgu_ref, bgu_ref, wd_ref, bd_ref, y_ref):
    d_exp = wd_ref.shape[1]

    @pl.when(pl.program_id(0) < nu_ref[0])
    def _():
        gu = _dot(x_ref[...].astype(BF16), wgu_ref[0]) + bgu_ref[0]
        gate = jnp.minimum(gu[:, :d_exp], SWIGLU_LIMIT)
        up = jnp.clip(gu[:, d_exp:], -SWIGLU_LIMIT, SWIGLU_LIMIT)
        glu = gate * jax.nn.sigmoid(SWIGLU_ALPHA * gate)
        y_ref[...] = _dot(((up + 1.0) * glu).astype(BF16), wd_ref[0]) + bd_ref[0]


def _experts(buf, block_e, n_used, w_gu, b_gu, w_down, b_down):
    n_pad, d = buf.shape
    d_exp = w_down.shape[1]
    used = lambda i, be, nu: jnp.minimum(i, nu[0] - 1)
    rows = pl.BlockSpec((EXPERT_BLOCK, d), lambda i, be, nu: (used(i, be, nu), 0))
    per_e = lambda *shape: pl.BlockSpec((1,) + shape, lambda i, be, nu: (be[used(i, be, nu)], 0, 0))
    return pl.pallas_call(
        _expert_kernel,
        grid_spec=pltpu.PrefetchScalarGridSpec(
            num_scalar_prefetch=2,
            grid=(n_pad // EXPERT_BLOCK,),
            in_specs=[rows, per_e(d, 2 * d_exp), per_e(1, 2 * d_exp), per_e(d_exp, d), per_e(1, d)],
            out_specs=rows,
        ),
        out_shape=jax.ShapeDtypeStruct((n_pad, d), F32),
        compiler_params=_params(1),
        name="experts",
    )(block_e, n_used, buf, w_gu, b_gu, w_down, b_down)


def _combine_kernel(dest_ref, tw_ref, x1_ref, p_ref, gp_ref, wg_ref, wp_ref, gfin_ref, y_ref, o_ref, rows_ref, sem,
                    *, final_norm):
    t = x1_ref.shape[0]

    def copy(r, kk):
        return pltpu.make_async_copy(y_ref.at[pl.ds(dest_ref[0, 0, TOP_K * r + kk], 1)],
                                     rows_ref.at[kk, pl.ds(r, 1)], sem)

    def start_rows(r, c):
        for kk in range(TOP_K):
            copy(r, kk).start()
        return c

    def wait_rows(r, c):
        for kk in range(TOP_K):
            copy(r, kk).wait()
        return c

    lax.fori_loop(0, t, start_rows, 0)
    lax.fori_loop(0, t, wait_rows, 0)

    tw = tw_ref[...]
    x2 = x1_ref[...]
    for kk in range(TOP_K):
        x2 = x2 + tw[:, kk:kk + 1] * rows_ref[kk]
    hp = _rms(x2, gp_ref[...]).astype(BF16)
    gate = jax.nn.sigmoid(_dot(hp, wg_ref[...]))
    x3 = x2 + gate * _dot(p_ref[...].astype(BF16), wp_ref[...])
    o_ref[...] = _rms(x3, gfin_ref[...]) if final_norm else x3


def _combine(dest3, top_w, x1, p2d, y_buf, g_ple, w_ple_gate, w_ple_proj, g_final, final_norm, t):
    n, d = x1.shape
    ple = p2d.shape[1]
    const = lambda *shape: pl.BlockSpec(shape, lambda i: (0,) * len(shape))
    tile = lambda w: pl.BlockSpec((t, w), lambda i: (i, 0))
    return pl.pallas_call(
        functools.partial(_combine_kernel, final_norm=final_norm),
        grid=(n // t,),
        in_specs=[pl.BlockSpec((1, 1, TOP_K * t), lambda i: (i, 0, 0), memory_space=pltpu.SMEM),
                  tile(LANES), tile(d), tile(ple), const(1, d), const(d, d), const(ple, d), const(1, d),
                  pl.BlockSpec(memory_space=pl.ANY)],
        out_specs=tile(d),
        out_shape=jax.ShapeDtypeStruct((n, d), F32),
        scratch_shapes=[pltpu.VMEM((TOP_K, t, d), F32), pltpu.SemaphoreType.DMA],
        compiler_params=_params(1),
        name="combine",
    )(dest3, top_w, x1, p2d, g_ple, w_ple_gate, w_ple_proj, g_final, y_buf)


def _layer(x, p, g_mix, w_in, w_pool_grp, pool_scale, w_pool_up, w_attn_up, w_out, g_ffn, w_router, b_router,
           w_gu, b_gu, w_down, b_down, g_ple, w_ple_gate, w_ple_proj, g_final, final_norm):
    b, s, d = x.shape
    n = b * s
    pool_w = len(POOL_WINDOWS) * POOL_CH
    att_w = N_HEADS * HEAD_DIM
    seq_tile = min(512, s)
    tok_tile = min(512, n)
    row_tile = min(256, n)
    attn_blk = min(256, s)

    row = lambda v: v.reshape(1, -1)
    w_in = w_in.astype(BF16)
    w_u, w_qkv = w_in[:, :pool_w], w_in[:, pool_w:pool_w + 3 * att_w]
    w_gp, w_ga = w_in[:, pool_w + 3 * att_w:pool_w + 3 * att_w + d], w_in[:, pool_w + 3 * att_w + d:]
    q, k, v, pool_branch, g_att = _mix_in(x, row(g_mix), w_u, w_qkv, w_gp, w_ga, w_pool_grp.astype(BF16),
                                          row(pool_scale), w_pool_up.astype(BF16), seq_tile)
    y_att = _attention(q, k, v, attn_blk)

    wr_hi = w_router.astype(BF16)
    wr_lo = (w_router - wr_hi.astype(F32)).astype(BF16)
    x1, h2, e_out, r_out, w_top, counts = _mix_out(
        x.reshape(n, d), y_att.reshape(n, att_w), pool_branch.reshape(n, d), g_att.reshape(n, d),
        w_attn_up.astype(BF16), w_out.astype(BF16), row(g_ffn), wr_hi, wr_lo, row(b_router), tok_tile)

    counts = counts[0].astype(jnp.int32)
    padded = (counts + EXPERT_BLOCK - 1) // EXPERT_BLOCK * EXPERT_BLOCK
    pad_ends = jnp.cumsum(padded)
    pad_starts = pad_ends - padded
    dest = pad_starts[e_out[:, :TOP_K]] + r_out[:, :TOP_K]
    n_blocks = -(-n * TOP_K // EXPERT_BLOCK) + N_EXPERTS
    block_e = jnp.minimum(jnp.searchsorted(pad_ends, jnp.arange(n_blocks) * EXPERT_BLOCK, side='right'),
                          N_EXPERTS - 1).astype(jnp.int32)
    n_used = (pad_ends[-1:] // EXPERT_BLOCK).astype(jnp.int32)
    dest3 = dest.reshape(n // row_tile, 1, TOP_K * row_tile)

    buf = _dispatch(h2, dest3, pad_ends, counts, n_blocks * EXPERT_BLOCK, row_tile)
    y_buf = _experts(buf, block_e, n_used, w_gu.astype(BF16), b_gu[:, None, :], w_down.astype(BF16),
                     b_down[:, None, :])
    out = _combine(dest3, w_top, x1, p.reshape(n, -1), y_buf, row(g_ple), w_ple_gate.astype(BF16),
                   w_ple_proj.astype(BF16), row(g_final), final_norm, row_tile)
    return out.reshape(b, s, d)


def kernel(x, p, g_mix, w_in, w_pool_grp, pool_scale, w_pool_up, w_attn_up, w_out, g_ffn, w_router, b_router,
           w_gu, b_gu, w_down, b_down, g_ple, w_ple_gate, w_ple_proj, g_final):
    depth = p.shape[0]
    for i in range(depth):
        x = _layer(x, p[i], g_mix[i], w_in[i], w_pool_grp[i], pool_scale[i], w_pool_up[i], w_attn_up[i], w_out[i],
                   g_ffn[i], w_router[i], b_router[i], w_gu[i], b_gu[i], w_down[i], b_down[i], g_ple[i],
                   w_ple_gate[i], w_ple_proj[i], g_final, i == depth - 1)
    return x
```

```python
import functools

import jax
import jax.numpy as jnp
from jax import lax
from jax.experimental import pallas as pl
from jax.experimental.pallas import tpu as pltpu

F32 = jnp.float32
BF16 = jnp.bfloat16

N_HEADS = 8
HEAD_DIM = 64
HEADS_PER_STEP = 2
ROW_CHUNK = 128
POOL_WINDOWS = (2, 4, 8, 16)
POOL_CH = 128
POOL_HIST = 16
N_EXPERTS = 32
TOP_K = 4
EXPERT_BLOCK = 512
SWIGLU_LIMIT = 7.0
SWIGLU_ALPHA = 1.702
EPS = 1e-6
LOG2_E = 1.4426950408889634
UNDERFLOW_BITS = 160.0
LANES = 128
VMEM_LIMIT = 56 * 1024 * 1024


def _params(n_axes, flags=None):
    return pltpu.CompilerParams(dimension_semantics=("arbitrary",) * n_axes, vmem_limit_bytes=VMEM_LIMIT,
                                flags=flags)


def _rms(xf, g):
    return xf * lax.rsqrt(jnp.mean(xf * xf, axis=-1, keepdims=True) + EPS) * g


def _dot(a, b):
    return jnp.dot(a, b, preferred_element_type=F32)


def _split_bf16(a):
    hi = a.astype(BF16)
    lo = (a - hi.astype(F32)).astype(BF16)
    return hi, lo


def _mix_in_kernel(x_ref, g_ref, wu_ref, wqkv_ref, wgp_ref, wga_ref, wgrp_ref, pscale_ref, wpu_ref,
                   q_ref, k_ref, v_ref, pb_ref, ga_ref, hist_ref):
    s = pl.program_id(1)
    ts = x_ref.shape[1]
    att_w = q_ref.shape[2]

    @pl.when(s == 0)
    def _():
        hist_ref[...] = jnp.zeros_like(hist_ref)

    h = _rms(x_ref[0], g_ref[...]).astype(BF16)
    u = _dot(h, wu_ref[...])
    ext = jnp.concatenate([hist_ref[...], u], axis=0)
    hist_ref[...] = u[ts - POOL_HIST:, :]
    t_pos = s * ts + lax.broadcasted_iota(jnp.int32, (ts, 1), 0)
    y_groups = []
    for g, w in enumerate(POOL_WINDOWS):
        a = ext[:, g * POOL_CH:(g + 1) * POOL_CH]
        sh = 1
        while sh < w:
            a = a + pltpu.roll(a, sh, axis=0)
            sh *= 2
        count = jnp.minimum(t_pos + 1, w).astype(F32)
        pooled = a[POOL_HIST:, :] / count - u[:, g * POOL_CH:(g + 1) * POOL_CH]
        y_groups.append(_dot(pooled.astype(BF16), wgrp_ref[g]))
    y_pool = jnp.concatenate(y_groups, axis=1) * pscale_ref[...]
    pool_up = _dot(y_pool.astype(BF16), wpu_ref[...])
    g_pool = jax.nn.sigmoid(_dot(h, wgp_ref[...]))
    pb_ref[0] = (g_pool * pool_up).astype(BF16)
    ga_ref[0] = jax.nn.sigmoid(_dot(h, wga_ref[...])).astype(BF16)
    qkv = _dot(h, wqkv_ref[...])
    q_ref[0] = (qkv[:, :att_w] * (HEAD_DIM ** -0.5 * LOG2_E)).astype(BF16)
    k_ref[0] = qkv[:, att_w:2 * att_w].astype(BF16)
    v_ref[0] = qkv[:, 2 * att_w:].astype(BF16)


def _mix_in(x, g_mix, w_u, w_qkv, w_gp, w_ga, w_grp, pool_scale, w_pool_up, ts):
    b, s, d = x.shape
    pool_w = w_u.shape[1]
    att_w = w_qkv.shape[1] // 3
    const = lambda *shape: pl.BlockSpec(shape, lambda i, j: (0,) * len(shape))
    tile = lambda w: pl.BlockSpec((1, ts, w), lambda i, j: (i, j, 0))
    return pl.pallas_call(
        _mix_in_kernel,
        grid=(b, s // ts),
        in_specs=[tile(d), const(1, d), const(d, pool_w), const(d, 3 * att_w), const(d, d), const(d, d),
                  const(len(POOL_WINDOWS), POOL_CH, POOL_CH), const(1, pool_w), const(pool_w, d)],
        out_specs=[tile(att_w), tile(att_w), tile(att_w), tile(d), tile(d)],
        out_shape=[jax.ShapeDtypeStruct((b, s, att_w), BF16)] * 3 + [jax.ShapeDtypeStruct((b, s, d), BF16)] * 2,
        scratch_shapes=[pltpu.VMEM((POOL_HIST, pool_w), F32)],
        compiler_params=_params(2),
        name="mix_in",
    )(x, g_mix, w_u, w_qkv, w_gp, w_ga, w_grp, pool_scale, w_pool_up)


def _attn_kernel(q_ref, k_ref, v_ref, tri_ref, o_ref, *, blk):
    i = pl.program_id(2)
    q2 = q_ref[0]
    lane = lax.broadcasted_iota(jnp.int32, (blk, LANES), 1)
    row = lax.broadcasted_iota(jnp.int32, (blk, blk), 0)
    col = lax.broadcasted_iota(jnp.int32, (blk, blk), 1)
    causal = col < row
    heads = (lane < HEAD_DIM, lane >= HEAD_DIM)
    qs = [jnp.where(hm, q2, jnp.zeros_like(q2)) for hm in heads]

    def keys_values(j):
        start = pl.multiple_of(j * blk, blk)
        return k_ref[0, pl.ds(start, blk), :], v_ref[0, pl.ds(start, blk), :]

    def scores(qh, kj, diagonal):
        z = lax.dot_general(qh, kj, (((1,), (1,)), ((), ())), preferred_element_type=F32)
        softplus = jnp.maximum(z, 0.0) + jnp.log2(1.0 + jnp.exp2(-jnp.abs(z)))
        if diagonal:
            softplus = jnp.where(causal, softplus, 0.0)
        hi, lo = _split_bf16(softplus)
        return z, _dot(jnp.concatenate([hi, lo], axis=1), tri_ref[...])

    def accumulate(z, incl, acc, run, vj, diagonal):
        w = jnp.exp2(z - incl - jnp.concatenate([run] * (blk // LANES), axis=1))
        if diagonal:
            w = jnp.where(causal, w, 0.0)
        return acc + _dot(w.astype(BF16), vj), run + jnp.broadcast_to(incl[:, 0:1], (blk, LANES))

    has_prev = i > 0
    kd, vd = keys_values(i)
    kp, vp = keys_values(jnp.maximum(i - 1, 0))
    front_d = [scores(qh, kd, True) for qh in qs]
    front_p = [scores(qh, kp, False) for qh in qs]
    zero = jnp.zeros((blk, LANES), F32)
    accs, runs = [], []
    for (zd, incl_d), (zp, incl_p) in zip(front_d, front_p):
        acc_d, run_d = accumulate(zd, incl_d, zero, zero, vd, True)
        acc_p, run_p = accumulate(zp, incl_p, acc_d, run_d, vp, False)
        accs.append(jnp.where(has_prev, acc_p, acc_d))
        runs.append(jnp.where(has_prev, run_p, run_d))

    def live(c):
        n, _, _, min_run = c
        return jnp.logical_and(n < i, min_run < UNDERFLOW_BITS)

    def step(c):
        n, accs, runs, _ = c
        kj, vj = keys_values(i - 1 - n)
        front = [scores(qh, kj, False) for qh in qs]
        back = [accumulate(z, incl, acc, run, vj, False) for (z, incl), acc, run in zip(front, accs, runs)]
        accs, runs = [a for a, _ in back], [r for _, r in back]
        return n + 1, accs, runs, jnp.minimum(jnp.min(runs[0]), jnp.min(runs[1]))

    min_run = jnp.minimum(jnp.min(runs[0]), jnp.min(runs[1]))
    _, accs, _, _ = lax.while_loop(live, step, (jnp.int32(1), accs, runs, min_run))
    o_ref[0] = jnp.where(heads[0], accs[0], accs[1]).astype(o_ref.dtype)


def _attention(q, k, v, blk):
    b, s, att_w = q.shape
    tri = (jnp.arange(blk)[:, None] >= jnp.arange(blk)[None, :]).astype(BF16)
    tri2 = jnp.concatenate([tri, tri], axis=0)
    qo_spec = pl.BlockSpec((1, blk, LANES), lambda bi, hp, i: (bi, i, hp))
    kv_spec = pl.BlockSpec((1, s, LANES), lambda bi, hp, i: (bi, 0, hp))
    return pl.pallas_call(
        functools.partial(_attn_kernel, blk=blk),
        grid=(b, att_w // LANES, s // blk),
        in_specs=[qo_spec, kv_spec, kv_spec, pl.BlockSpec((2 * blk, blk), lambda bi, hp, i: (0, 0))],
        out_specs=qo_spec,
        out_shape=jax.ShapeDtypeStruct((b, s, att_w), BF16),
        compiler_params=_params(3),
        name="attn",
    )(q, k, v, tri2)


def _mix_out_kernel(x_ref, ya_ref, pb_ref, ga_ref, wau_ref, wo_ref, gf_ref, wrh_ref, wrl_ref, br_ref, lt_ref,
                    x1_ref, h2_ref, e_ref, r_ref, w_ref, cnt_ref):
    t = x_ref.shape[0]

    @pl.when(pl.program_id(0) == 0)
    def _():
        cnt_ref[...] = jnp.zeros_like(cnt_ref)

    att_up = _dot(ya_ref[...], wau_ref[...])
    merged = pb_ref[...].astype(F32) + ga_ref[...].astype(F32) * att_up
    x1 = x_ref[...] + _dot(merged.astype(BF16), wo_ref[...])
    x1_ref[...] = x1
    h2 = _rms(x1, gf_ref[...])
    h2_ref[...] = h2

    hi, lo = _split_bf16(h2)
    logits = _dot(hi, wrh_ref[...]) + _dot(lo, wrh_ref[...]) + _dot(hi, wrl_ref[...]) + br_ref[...]
    lane_e = lax.broadcasted_iota(jnp.int32, (t, N_EXPERTS), 1)
    vals, idxs = [], []
    for _ in range(TOP_K):
        m = jnp.max(logits, axis=-1, keepdims=True)
        idx = jnp.min(jnp.where(logits == m, lane_e, N_EXPERTS), axis=-1, keepdims=True)
        vals.append(m)
        idxs.append(idx)
        logits = jnp.where(lane_e == idx, -jnp.inf, logits)
    exps = [jnp.exp(m - vals[0]) for m in vals]
    denom = exps[0] + exps[1] + exps[2] + exps[3]

    onehot = jnp.zeros((t, N_EXPERTS), F32)
    for idx in idxs:
        onehot = onehot + jnp.where(lane_e == idx, 1.0, 0.0)
    pos = cnt_ref[...] + _dot(lt_ref[...], onehot.astype(BF16))
    cnt_ref[...] = cnt_ref[...] + jnp.sum(onehot, axis=0, keepdims=True)

    lane = lax.broadcasted_iota(jnp.int32, (t, LANES), 1)
    e_out = jnp.zeros((t, LANES), jnp.int32)
    r_out = jnp.zeros((t, LANES), jnp.int32)
    w_out = jnp.zeros((t, LANES), F32)
    for kk in range(TOP_K):
        rank = jnp.sum(jnp.where(lane_e == idxs[kk], pos, 0.0), axis=-1, keepdims=True).astype(jnp.int32)
        e_out = jnp.where(lane == kk, idxs[kk], e_out)
        r_out = jnp.where(lane == kk, rank, r_out)
        w_out = jnp.where(lane == kk, exps[kk] / denom, w_out)
    e_ref[...] = e_out
    r_ref[...] = r_out
    w_ref[...] = w_out


def _mix_out(x2d, y_att, pool_branch, g_att, w_attn_up, w_out, g_ffn, wr_hi, wr_lo, b_router, t):
    n, d = x2d.shape
    att_w = y_att.shape[1]
    lower = (jnp.arange(t)[:, None] > jnp.arange(t)[None, :]).astype(BF16)
    const = lambda *shape: pl.BlockSpec(shape, lambda i: (0,) * len(shape))
    tile = lambda w: pl.BlockSpec((t, w), lambda i: (i, 0))
    return pl.pallas_call(
        _mix_out_kernel,
        grid=(n // t,),
        in_specs=[tile(d), tile(att_w), tile(d), tile(d), const(att_w, d), const(d, d), const(1, d),
                  const(d, N_EXPERTS), const(d, N_EXPERTS), const(1, N_EXPERTS), const(t, t)],
        out_specs=[tile(d), tile(d), tile(LANES), tile(LANES), tile(LANES), const(1, N_EXPERTS)],
        out_shape=[jax.ShapeDtypeStruct((n, d), F32), jax.ShapeDtypeStruct((n, d), F32),
                   jax.ShapeDtypeStruct((n, LANES), jnp.int32), jax.ShapeDtypeStruct((n, LANES), jnp.int32),
                   jax.ShapeDtypeStruct((n, LANES), F32), jax.ShapeDtypeStruct((1, N_EXPERTS), F32)],
        compiler_params=_params(1),
        name="mix_out",
    )(x2d, y_att, pool_branch, g_att, w_attn_up, w_out, g_ffn, wr_hi, wr_lo, b_router, lower)


def _row_copy(src_ref, src_row, dst_ref, dst_row, sem):
    return pltpu.make_async_copy(src_ref.at[pl.ds(src_row, 1)], dst_ref.at[pl.ds(dst_row, 1)], sem)


def _dispatch_kernel(pe_ref, cnt_ref, dest_ref, h_ref, buf_ref, zero_ref, sem, zsem):
    t = h_ref.shape[0]
    n_blocks = buf_ref.shape[0] // EXPERT_BLOCK

    @pl.when(pl.program_id(0) == 0)
    def _():
        zero_ref[...] = jnp.zeros_like(zero_ref)

        def zero_copy(start):
            start = pl.multiple_of(start, EXPERT_BLOCK)
            return pltpu.make_async_copy(zero_ref, buf_ref.at[pl.ds(start, EXPERT_BLOCK)], zsem)

        def start_last(e, c):
            @pl.when(cnt_ref[e] > 0)
            def _():
                zero_copy(pe_ref[e] - EXPERT_BLOCK).start()
            return c

        def wait_last(e, c):
            @pl.when(cnt_ref[e] > 0)
            def _():
                zero_copy(pe_ref[e] - EXPERT_BLOCK).wait()
            return c

        def start_tail(b, c):
            zero_copy(b * EXPERT_BLOCK).start()
            return c

        def wait_tail(b, c):
            zero_copy(b * EXPERT_BLOCK).wait()
            return c

        n_used = pe_ref[N_EXPERTS - 1] // EXPERT_BLOCK
        lax.fori_loop(0, N_EXPERTS, start_last, 0)
        lax.fori_loop(n_used, n_blocks, start_tail, 0)
        lax.fori_loop(0, N_EXPERTS, wait_last, 0)
        lax.fori_loop(n_used, n_blocks, wait_tail, 0)

    def start_rows(r, c):
        for kk in range(TOP_K):
            _row_copy(h_ref, r, buf_ref, dest_ref[0, 0, TOP_K * r + kk], sem).start(priority=kk % 2)
        return c

    lax.fori_loop(0, t, start_rows, 0)
    for kk in range(TOP_K):
        pltpu.make_async_copy(h_ref, buf_ref.at[pl.ds(0, t)], sem).wait()


def _dispatch(h2, dest3, pad_ends, counts, n_pad, t):
    n, d = h2.shape
    return pl.pallas_call(
        _dispatch_kernel,
        grid_spec=pltpu.PrefetchScalarGridSpec(
            num_scalar_prefetch=2,
            grid=(n // t,),
            in_specs=[pl.BlockSpec((1, 1, TOP_K * t), lambda i, pe, cn: (i, 0, 0), memory_space=pltpu.SMEM),
                      pl.BlockSpec((t, d), lambda i, pe, cn: (i, 0))],
            out_specs=pl.BlockSpec(memory_space=pl.ANY),
            scratch_shapes=[pltpu.VMEM((EXPERT_BLOCK, d), h2.dtype), pltpu.SemaphoreType.DMA,
                            pltpu.SemaphoreType.DMA],
        ),
        out_shape=jax.ShapeDtypeStruct((n_pad, d), h2.dtype),
        compiler_params=_params(1),
        name="dispatch",
    )(pad_ends, counts, dest3, h2)


def _expert_kernel(be_ref, nu_ref, x_ref, wgu_ref, bgu_ref, wd_ref, bd_ref, y_ref, wgu_bf, wd_bf):
    d_exp = wd_ref.shape[1]
    i = pl.program_id(0)

    @pl.when(jnp.logical_or(i == 0, be_ref[i] != be_ref[jnp.maximum(i - 1, 0)]))
    def _():
        wgu_bf[...] = wgu_ref[0].astype(BF16)
        wd_bf[...] = wd_ref[0].astype(BF16)

    @pl.when(i < nu_ref[0])
    def _():
        gu = _dot(x_ref[...].astype(BF16), wgu_bf[...]) + bgu_ref[0]
        gate = jnp.minimum(gu[:, :d_exp], SWIGLU_LIMIT)
        up = jnp.clip(gu[:, d_exp:], -SWIGLU_LIMIT, SWIGLU_LIMIT)
        glu = gate * jax.nn.sigmoid(SWIGLU_ALPHA * gate)
        y_ref[...] = _dot(((up + 1.0) * glu).astype(BF16), wd_bf[...]) + bd_ref[0]

    @pl.when(i >= nu_ref[0])
    def _():
        y_ref[...] = jnp.zeros_like(y_ref)


def _experts(buf, block_e, n_used, w_gu, b_gu, w_down, b_down):
    n_pad, d = buf.shape
    d_exp = w_down.shape[1]
    used = lambda i, be, nu: jnp.minimum(i, nu[0] - 1)
    rows = pl.BlockSpec((EXPERT_BLOCK, d), lambda i, be, nu: (used(i, be, nu), 0))
    out_rows = pl.BlockSpec((EXPERT_BLOCK, d), lambda i, be, nu: (i, 0))
    per_e = lambda *shape: pl.BlockSpec((1,) + shape, lambda i, be, nu: (be[used(i, be, nu)], 0, 0))
    return pl.pallas_call(
        _expert_kernel,
        grid_spec=pltpu.PrefetchScalarGridSpec(
            num_scalar_prefetch=2,
            grid=(n_pad // EXPERT_BLOCK,),
            in_specs=[rows, per_e(d, 2 * d_exp), per_e(1, 2 * d_exp), per_e(d_exp, d), per_e(1, d)],
            out_specs=out_rows,
            scratch_shapes=[pltpu.VMEM((d, 2 * d_exp), BF16), pltpu.VMEM((d_exp, d), BF16)],
        ),
        out_shape=jax.ShapeDtypeStruct((n_pad, d), F32),
        compiler_params=_params(1),
        name="experts",
    )(block_e, n_used, buf, w_gu, b_gu, w_down, b_down)


def _combine_kernel(dest_ref, next_ref, tw_ref, x1_ref, p_ref, gp_ref, wg_ref, wp_ref, gfin_ref, y_ref, o_ref,
                    rows_ref, sems, *, final_norm):
    t = x1_ref.shape[0]
    i = pl.program_id(0)
    slot = lax.rem(i, 2)

    def gather(idx_ref, into):
        def start_rows(r, c):
            for kk in range(TOP_K):
                pltpu.make_async_copy(y_ref.at[pl.ds(idx_ref[0, 0, TOP_K * r + kk], 1)],
                                      rows_ref.at[into, kk, pl.ds(r, 1)], sems.at[into]).start(priority=kk % 2)
            return c
        lax.fori_loop(0, t, start_rows, 0)

    @pl.when(i == 0)
    def _():
        gather(dest_ref, 0)

    @pl.when(i + 1 < pl.num_programs(0))
    def _():
        gather(next_ref, 1 - slot)

    for kk in range(TOP_K):
        pltpu.make_async_copy(y_ref.at[pl.ds(0, t)], rows_ref.at[slot, kk], sems.at[slot]).wait()

    tw = tw_ref[...]
    x2 = x1_ref[...]
    for kk in range(TOP_K):
        x2 = x2 + tw[:, kk:kk + 1] * rows_ref[slot, kk]
    hp = _rms(x2, gp_ref[...]).astype(BF16)
    gate = jax.nn.sigmoid(_dot(hp, wg_ref[...]))
    x3 = x2 + gate * _dot(p_ref[...].astype(BF16), wp_ref[...])
    o_ref[...] = _rms(x3, gfin_ref[...]) if final_norm else x3


def _combine(dest3, top_w, x1, p2d, y_buf, g_ple, w_ple_gate, w_ple_proj, g_final, final_norm, t):
    n, d = x1.shape
    ple = p2d.shape[1]
    const = lambda *shape: pl.BlockSpec(shape, lambda i: (0,) * len(shape))
    tile = lambda w: pl.BlockSpec((t, w), lambda i: (i, 0))
    last = n // t - 1
    return pl.pallas_call(
        functools.partial(_combine_kernel, final_norm=final_norm),
        grid=(n // t,),
        in_specs=[pl.BlockSpec((1, 1, TOP_K * t), lambda i: (i, 0, 0), memory_space=pltpu.SMEM),
                  pl.BlockSpec((1, 1, TOP_K * t), lambda i: (jnp.minimum(i + 1, last), 0, 0),
                               memory_space=pltpu.SMEM),
                  tile(LANES), tile(d), tile(ple), const(1, d), const(d, d), const(ple, d), const(1, d),
                  pl.BlockSpec(memory_space=pl.ANY)],
        out_specs=tile(d),
        out_shape=jax.ShapeDtypeStruct((n, d), F32),
        scratch_shapes=[pltpu.VMEM((2, TOP_K, t, d), F32), pltpu.SemaphoreType.DMA((2,))],
        compiler_params=_params(1),
        name="combine",
    )(dest3, dest3, top_w, x1, p2d, g_ple, w_ple_gate, w_ple_proj, g_final, y_buf)


def _layer(x, p, g_mix, w_in, w_pool_grp, pool_scale, w_pool_up, w_attn_up, w_out, g_ffn, w_router, b_router,
           w_gu, b_gu, w_down, b_down, g_ple, w_ple_gate, w_ple_proj, g_final, final_norm):
    b, s, d = x.shape
    n = b * s
    pool_w = len(POOL_WINDOWS) * POOL_CH
    att_w = N_HEADS * HEAD_DIM
    seq_tile = min(512, s)
    tok_tile = min(512, n)
    row_tile = min(256, n)
    attn_blk = min(256, s)

    row = lambda v: v.reshape(1, -1)
    w_in = w_in.astype(BF16)
    w_u, w_qkv = w_in[:, :pool_w], w_in[:, pool_w:pool_w + 3 * att_w]
    w_gp, w_ga = w_in[:, pool_w + 3 * att_w:pool_w + 3 * att_w + d], w_in[:, pool_w + 3 * att_w + d:]
    q, k, v, pool_branch, g_att = _mix_in(x, row(g_mix), w_u, w_qkv, w_gp, w_ga, w_pool_grp.astype(BF16),
                                          row(pool_scale), w_pool_up.astype(BF16), seq_tile)
    y_att = _attention(q, k, v, attn_blk)

    wr_hi = w_router.astype(BF16)
    wr_lo = (w_router - wr_hi.astype(F32)).astype(BF16)
    x1, h2, e_out, r_out, w_top, counts = _mix_out(
        x.reshape(n, d), y_att.reshape(n, att_w), pool_branch.reshape(n, d), g_att.reshape(n, d),
        w_attn_up.astype(BF16), w_out.astype(BF16), row(g_ffn), wr_hi, wr_lo, row(b_router), tok_tile)

    counts = counts[0].astype(jnp.int32)
    padded = (counts + EXPERT_BLOCK - 1) // EXPERT_BLOCK * EXPERT_BLOCK
    pad_ends = jnp.cumsum(padded)
    pad_starts = pad_ends - padded
    dest = pad_starts[e_out[:, :TOP_K]] + r_out[:, :TOP_K]
    n_blocks = -(-n * TOP_K // EXPERT_BLOCK) + N_EXPERTS
    block_starts = jnp.arange(n_blocks, dtype=jnp.int32) * EXPERT_BLOCK
    block_e = jnp.minimum(jnp.sum((pad_ends[None, :] <= block_starts[:, None]).astype(jnp.int32), axis=1),
                          N_EXPERTS - 1)
    n_used = (pad_ends[-1:] // EXPERT_BLOCK).astype(jnp.int32)
    dest3 = dest.reshape(n // row_tile, 1, TOP_K * row_tile)

    buf = _dispatch(h2, dest3, pad_ends, counts, n_blocks * EXPERT_BLOCK, row_tile)
    y_buf = _experts(buf, block_e, n_used, w_gu, b_gu[:, None, :], w_down, b_down[:, None, :])
    out = _combine(dest3, w_top, x1, p.reshape(n, -1), y_buf, row(g_ple), w_ple_gate.astype(BF16),
                   w_ple_proj.astype(BF16), row(g_final), final_norm, row_tile)
    return out.reshape(b, s, d)


def kernel(x, p, g_mix, w_in, w_pool_grp, pool_scale, w_pool_up, w_attn_up, w_out, g_ffn, w_router, b_router,
           w_gu, b_gu, w_down, b_down, g_ple, w_ple_gate, w_ple_proj, g_final):
    depth = p.shape[0]
    for i in range(depth):
        x = _layer(x, p[i], g_mix[i], w_in[i], w_pool_grp[i], pool_scale[i], w_pool_up[i], w_attn_up[i], w_out[i],
                   g_ffn[i], w_router[i], b_router[i], w_gu[i], b_gu[i], w_down[i], b_down[i], g_ple[i],
                   w_ple_gate[i], w_ple_proj[i], g_final, i == depth - 1)
    return x
```

```python
import functools

import jax
import jax.numpy as jnp
from jax import lax
from jax.experimental import pallas as pl
from jax.experimental.pallas import tpu as pltpu

F32 = jnp.float32
BF16 = jnp.bfloat16

N_HEADS = 8
HEAD_DIM = 64
HEADS_PER_STEP = 2
ROW_CHUNK = 128
POOL_WINDOWS = (2, 4, 8, 16)
POOL_CH = 128
POOL_HIST = 16
N_EXPERTS = 32
TOP_K = 4
EXPERT_BLOCK = 512
SEG_ALIGN = 8
BIG_CHUNK = 32
SWIGLU_LIMIT = 7.0
SWIGLU_ALPHA = 1.702
EPS = 1e-6
LOG2_E = 1.4426950408889634
UNDERFLOW_BITS = 160.0
LANES = 128
VMEM_LIMIT = 56 * 1024 * 1024


def _params(n_axes, flags=None):
    return pltpu.CompilerParams(dimension_semantics=("arbitrary",) * n_axes, vmem_limit_bytes=VMEM_LIMIT,
                                flags=flags)


def _rms(xf, g):
    return xf * lax.rsqrt(jnp.mean(xf * xf, axis=-1, keepdims=True) + EPS) * g


def _dot(a, b):
    return jnp.dot(a, b, preferred_element_type=F32)


def _split_bf16(a):
    hi = a.astype(BF16)
    lo = (a - hi.astype(F32)).astype(BF16)
    return hi, lo


def _mix_in_kernel(x_ref, g_ref, wu_ref, wqkv_ref, wgp_ref, wga_ref, wgrp_ref, pscale_ref, wpu_ref,
                   q_ref, k_ref, v_ref, pb_ref, ga_ref, hist_ref):
    s = pl.program_id(1)
    ts = x_ref.shape[1]
    att_w = q_ref.shape[2]

    @pl.when(s == 0)
    def _():
        hist_ref[...] = jnp.zeros_like(hist_ref)

    h = _rms(x_ref[0], g_ref[...]).astype(BF16)
    u = _dot(h, wu_ref[...])
    ext = jnp.concatenate([hist_ref[...], u], axis=0)
    hist_ref[...] = u[ts - POOL_HIST:, :]
    t_pos = s * ts + lax.broadcasted_iota(jnp.int32, (ts, 1), 0)
    y_groups = []
    for g, w in enumerate(POOL_WINDOWS):
        a = ext[:, g * POOL_CH:(g + 1) * POOL_CH]
        sh = 1
        while sh < w:
            a = a + pltpu.roll(a, sh, axis=0)
            sh *= 2
        count = jnp.minimum(t_pos + 1, w).astype(F32)
        pooled = a[POOL_HIST:, :] / count - u[:, g * POOL_CH:(g + 1) * POOL_CH]
        y_groups.append(_dot(pooled.astype(BF16), wgrp_ref[g]))
    y_pool = jnp.concatenate(y_groups, axis=1) * pscale_ref[...]
    pool_up = _dot(y_pool.astype(BF16), wpu_ref[...])
    g_pool = jax.nn.sigmoid(_dot(h, wgp_ref[...]))
    pb_ref[0] = (g_pool * pool_up).astype(BF16)
    ga_ref[0] = jax.nn.sigmoid(_dot(h, wga_ref[...])).astype(BF16)
    qkv = _dot(h, wqkv_ref[...])
    q_ref[0] = (qkv[:, :att_w] * (HEAD_DIM ** -0.5 * LOG2_E)).astype(BF16)
    k_ref[0] = qkv[:, att_w:2 * att_w].astype(BF16)
    v_ref[0] = qkv[:, 2 * att_w:].astype(BF16)


def _mix_in(x, g_mix, w_u, w_qkv, w_gp, w_ga, w_grp, pool_scale, w_pool_up, ts):
    b, s, d = x.shape
    pool_w = w_u.shape[1]
    att_w = w_qkv.shape[1] // 3
    const = lambda *shape: pl.BlockSpec(shape, lambda i, j: (0,) * len(shape))
    tile = lambda w: pl.BlockSpec((1, ts, w), lambda i, j: (i, j, 0))
    return pl.pallas_call(
        _mix_in_kernel,
        grid=(b, s // ts),
        in_specs=[tile(d), const(1, d), const(d, pool_w), const(d, 3 * att_w), const(d, d), const(d, d),
                  const(len(POOL_WINDOWS), POOL_CH, POOL_CH), const(1, pool_w), const(pool_w, d)],
        out_specs=[tile(att_w), tile(att_w), tile(att_w), tile(d), tile(d)],
        out_shape=[jax.ShapeDtypeStruct((b, s, att_w), BF16)] * 3 + [jax.ShapeDtypeStruct((b, s, d), BF16)] * 2,
        scratch_shapes=[pltpu.VMEM((POOL_HIST, pool_w), F32)],
        compiler_params=_params(2),
        name="mix_in",
    )(x, g_mix, w_u, w_qkv, w_gp, w_ga, w_grp, pool_scale, w_pool_up)


def _attn_kernel(q_ref, k_ref, v_ref, tri_ref, o_ref, *, blk):
    i = pl.program_id(2)
    q2 = q_ref[0]
    lane = lax.broadcasted_iota(jnp.int32, (blk, LANES), 1)
    row = lax.broadcasted_iota(jnp.int32, (blk, blk), 0)
    col = lax.broadcasted_iota(jnp.int32, (blk, blk), 1)
    causal = col < row
    heads = (lane < HEAD_DIM, lane >= HEAD_DIM)
    qs = [jnp.where(hm, q2, jnp.zeros_like(q2)) for hm in heads]

    def keys_values(j):
        start = pl.multiple_of(j * blk, blk)
        return k_ref[0, pl.ds(start, blk), :], v_ref[0, pl.ds(start, blk), :]

    def scores(qh, kj, diagonal):
        z = lax.dot_general(qh, kj, (((1,), (1,)), ((), ())), preferred_element_type=F32)
        softplus = jnp.maximum(z, 0.0) + jnp.log2(1.0 + jnp.exp2(-jnp.abs(z)))
        if diagonal:
            softplus = jnp.where(causal, softplus, 0.0)
        hi, lo = _split_bf16(softplus)
        return z, _dot(jnp.concatenate([hi, lo], axis=1), tri_ref[...])

    def accumulate(z, incl, acc, run, vj, diagonal):
        w = jnp.exp2(z - incl - jnp.concatenate([run] * (blk // LANES), axis=1))
        if diagonal:
            w = jnp.where(causal, w, 0.0)
        return acc + _dot(w.astype(BF16), vj), run + jnp.broadcast_to(incl[:, 0:1], (blk, LANES))

    has_prev = i > 0
    kd, vd = keys_values(i)
    kp, vp = keys_values(jnp.maximum(i - 1, 0))
    front_d = [scores(qh, kd, True) for qh in qs]
    front_p = [scores(qh, kp, False) for qh in qs]
    zero = jnp.zeros((blk, LANES), F32)
    accs, runs = [], []
    for (zd, incl_d), (zp, incl_p) in zip(front_d, front_p):
        acc_d, run_d = accumulate(zd, incl_d, zero, zero, vd, True)
        acc_p, run_p = accumulate(zp, incl_p, acc_d, run_d, vp, False)
        accs.append(jnp.where(has_prev, acc_p, acc_d))
        runs.append(jnp.where(has_prev, run_p, run_d))

    def live(c):
        n, _, _, min_run = c
        return jnp.logical_and(n < i, min_run < UNDERFLOW_BITS)

    def step(c):
        n, accs, runs, _ = c
        kj, vj = keys_values(i - 1 - n)
        front = [scores(qh, kj, False) for qh in qs]
        back = [accumulate(z, incl, acc, run, vj, False) for (z, incl), acc, run in zip(front, accs, runs)]
        accs, runs = [a for a, _ in back], [r for _, r in back]
        return n + 1, accs, runs, jnp.minimum(jnp.min(runs[0]), jnp.min(runs[1]))

    min_run = jnp.minimum(jnp.min(runs[0]), jnp.min(runs[1]))
    _, accs, _, _ = lax.while_loop(live, step, (jnp.int32(1), accs, runs, min_run))
    o_ref[0] = jnp.where(heads[0], accs[0], accs[1]).astype(o_ref.dtype)


def _attention(q, k, v, blk):
    b, s, att_w = q.shape
    tri = (jnp.arange(blk)[:, None] >= jnp.arange(blk)[None, :]).astype(BF16)
    tri2 = jnp.concatenate([tri, tri], axis=0)
    qo_spec = pl.BlockSpec((1, blk, LANES), lambda bi, hp, i: (bi, i, hp))
    kv_spec = pl.BlockSpec((1, s, LANES), lambda bi, hp, i: (bi, 0, hp))
    return pl.pallas_call(
        functools.partial(_attn_kernel, blk=blk),
        grid=(b, att_w // LANES, s // blk),
        in_specs=[qo_spec, kv_spec, kv_spec, pl.BlockSpec((2 * blk, blk), lambda bi, hp, i: (0, 0))],
        out_specs=qo_spec,
        out_shape=jax.ShapeDtypeStruct((b, s, att_w), BF16),
        compiler_params=_params(3),
        name="attn",
    )(q, k, v, tri2)


def _mix_out_kernel(x_ref, ya_ref, pb_ref, ga_ref, wau_ref, wo_ref, gf_ref, wrh_ref, wrl_ref, br_ref, lt_ref,
                    x1_ref, h2_ref, e_ref, r_ref, w_ref, cnt_ref):
    t = x_ref.shape[0]

    att_up = _dot(ya_ref[...], wau_ref[...])
    merged = pb_ref[...].astype(F32) + ga_ref[...].astype(F32) * att_up
    x1 = x_ref[...] + _dot(merged.astype(BF16), wo_ref[...])
    x1_ref[...] = x1
    h2 = _rms(x1, gf_ref[...])
    h2_ref[...] = h2.astype(h2_ref.dtype)

    hi, lo = _split_bf16(h2)
    logits = _dot(hi, wrh_ref[...]) + _dot(lo, wrh_ref[...]) + _dot(hi, wrl_ref[...]) + br_ref[...]
    lane_e = lax.broadcasted_iota(jnp.int32, (t, N_EXPERTS), 1)
    vals, idxs = [], []
    for _ in range(TOP_K):
        m = jnp.max(logits, axis=-1, keepdims=True)
        idx = jnp.min(jnp.where(logits == m, lane_e, N_EXPERTS), axis=-1, keepdims=True)
        vals.append(m)
        idxs.append(idx)
        logits = jnp.where(lane_e == idx, -jnp.inf, logits)
    exps = [jnp.exp(m - vals[0]) for m in vals]
    denom = exps[0] + exps[1] + exps[2] + exps[3]

    onehot = jnp.zeros((t, N_EXPERTS), F32)
    for idx in idxs:
        onehot = onehot + jnp.where(lane_e == idx, 1.0, 0.0)
    pos = _dot(lt_ref[...], onehot.astype(BF16))
    sort_tile = t // cnt_ref.shape[0]
    for j in range(cnt_ref.shape[0]):
        cnt_ref[j] = jnp.sum(onehot[j * sort_tile:(j + 1) * sort_tile], axis=0, keepdims=True)

    lane = lax.broadcasted_iota(jnp.int32, (t, LANES), 1)
    e_out = jnp.zeros((t, LANES), jnp.int32)
    r_out = jnp.zeros((t, LANES), jnp.int32)
    w_out = jnp.zeros((t, LANES), F32)
    for kk in range(TOP_K):
        rank = jnp.sum(jnp.where(lane_e == idxs[kk], pos, 0.0), axis=-1, keepdims=True).astype(jnp.int32)
        e_out = jnp.where(lane == kk, idxs[kk], e_out)
        r_out = jnp.where(lane == kk, rank, r_out)
        w_out = jnp.where(lane == kk, exps[kk] / denom, w_out)
    e_ref[...] = e_out
    r_ref[...] = r_out
    w_ref[...] = w_out


def _mix_out(x2d, y_att, pool_branch, g_att, w_attn_up, w_out, g_ffn, wr_hi, wr_lo, b_router, t, sort_tile):
    n, d = x2d.shape
    att_w = y_att.shape[1]
    r, c = jnp.arange(t)[:, None], jnp.arange(t)[None, :]
    lower = jnp.logical_and(r > c, r // sort_tile == c // sort_tile).astype(BF16)
    const = lambda *shape: pl.BlockSpec(shape, lambda i: (0,) * len(shape))
    tile = lambda w: pl.BlockSpec((t, w), lambda i: (i, 0))
    return pl.pallas_call(
        _mix_out_kernel,
        grid=(n // t,),
        in_specs=[tile(d), tile(att_w), tile(d), tile(d), const(att_w, d), const(d, d), const(1, d),
                  const(d, N_EXPERTS), const(d, N_EXPERTS), const(1, N_EXPERTS), const(t, t)],
        out_specs=[tile(d), tile(d), tile(LANES), tile(LANES), tile(LANES),
                   pl.BlockSpec((t // sort_tile, 1, N_EXPERTS), lambda i: (i, 0, 0))],
        out_shape=[jax.ShapeDtypeStruct((n, d), F32), jax.ShapeDtypeStruct((n, d), BF16),
                   jax.ShapeDtypeStruct((n, LANES), jnp.int32), jax.ShapeDtypeStruct((n, LANES), jnp.int32),
                   jax.ShapeDtypeStruct((n, LANES), F32),
                   jax.ShapeDtypeStruct((n // sort_tile, 1, N_EXPERTS), F32)],
        compiler_params=_params(1),
        name="mix_out",
    )(x2d, y_att, pool_branch, g_att, w_attn_up, w_out, g_ffn, wr_hi, wr_lo, b_router, lower)


def _for_segment_chunks(n8_ref, so_ref, go_ref, tile, act):
    def segment(e, c):
        n8 = n8_ref[tile * N_EXPERTS + e]
        in_tile = so_ref[tile * N_EXPERTS + e]
        in_buf = go_ref[tile * N_EXPERTS + e]
        n_big = lax.div(n8, BIG_CHUNK)
        done = n_big * BIG_CHUNK

        def big(j, c2):
            act(pl.multiple_of(in_tile + j * BIG_CHUNK, SEG_ALIGN), pl.multiple_of(in_buf + j * BIG_CHUNK, SEG_ALIGN),
                BIG_CHUNK)
            return c2

        def small(j, c2):
            act(pl.multiple_of(in_tile + done + j * SEG_ALIGN, SEG_ALIGN),
                pl.multiple_of(in_buf + done + j * SEG_ALIGN, SEG_ALIGN), SEG_ALIGN)
            return c2

        lax.fori_loop(0, n_big, big, 0)
        lax.fori_loop(0, lax.div(n8 - done, SEG_ALIGN), small, 0)
        return c

    lax.fori_loop(0, N_EXPERTS, segment, 0)


def _dispatch_kernel(pe_ref, cnt_ref, n8_ref, so_ref, go_ref, slot_ref, h_ref, buf_ref, zero_ref, sorted_ref, sems,
                     zsem):
    t = h_ref.shape[0]
    n_slots = sorted_ref.shape[1]
    n_blocks = buf_ref.shape[0] // EXPERT_BLOCK
    i = pl.program_id(0)
    cur = lax.rem(i, 2)

    def segment_copies(tile, into, act):
        def chunk(in_tile, in_buf, rows):
            act(pltpu.make_async_copy(sorted_ref.at[into, pl.ds(in_tile, rows)], buf_ref.at[pl.ds(in_buf, rows)],
                                      sems.at[into]))
        _for_segment_chunks(n8_ref, so_ref, go_ref, tile, chunk)

    @pl.when(i == 0)
    def _():
        zero_ref[...] = jnp.zeros_like(zero_ref)

        def zero_copy(start):
            start = pl.multiple_of(start, EXPERT_BLOCK)
            return pltpu.make_async_copy(zero_ref, buf_ref.at[pl.ds(start, EXPERT_BLOCK)], zsem)

        def start_last(e, c):
            @pl.when(cnt_ref[e] > 0)
            def _():
                zero_copy(pe_ref[e] - EXPERT_BLOCK).start()
            return c

        def wait_last(e, c):
            @pl.when(cnt_ref[e] > 0)
            def _():
                zero_copy(pe_ref[e] - EXPERT_BLOCK).wait()
            return c

        def start_tail(b, c):
            zero_copy(b * EXPERT_BLOCK).start()
            return c

        def wait_tail(b, c):
            zero_copy(b * EXPERT_BLOCK).wait()
            return c

        n_used = pe_ref[N_EXPERTS - 1] // EXPERT_BLOCK
        lax.fori_loop(0, N_EXPERTS, start_last, 0)
        lax.fori_loop(n_used, n_blocks, start_tail, 0)
        lax.fori_loop(0, N_EXPERTS, wait_last, 0)
        lax.fori_loop(n_used, n_blocks, wait_tail, 0)

    @pl.when(i >= 2)
    def _():
        segment_copies(i - 2, cur, lambda c: c.wait())

    slots = slot_ref[...]
    slot_id = lax.broadcasted_iota(jnp.int32, (t, n_slots), 1)
    hit = slot_id == slots[:, 0:1]
    for kk in range(1, TOP_K):
        hit = jnp.logical_or(hit, slot_id == slots[:, kk:kk + 1])
    select = jnp.where(hit, 1.0, 0.0).astype(BF16)
    sorted_ref[cur] = lax.dot_general(select, h_ref[...], (((0,), (0,)), ((), ())), preferred_element_type=F32)
    segment_copies(i, cur, lambda c: c.start())

    @pl.when(i == pl.num_programs(0) - 1)
    def _():
        @pl.when(i >= 1)
        def _():
            segment_copies(i - 1, 1 - cur, lambda c: c.wait())
        segment_copies(i, cur, lambda c: c.wait())


def _dispatch(h2, slots, pad_ends, counts, seg_rows, seg_src, seg_dst, n_pad, t):
    n, d = h2.shape
    n_slots = TOP_K * t + N_EXPERTS * SEG_ALIGN
    return pl.pallas_call(
        _dispatch_kernel,
        grid_spec=pltpu.PrefetchScalarGridSpec(
            num_scalar_prefetch=5,
            grid=(n // t,),
            in_specs=[pl.BlockSpec((t, LANES), lambda i, *_: (i, 0)), pl.BlockSpec((t, d), lambda i, *_: (i, 0))],
            out_specs=pl.BlockSpec(memory_space=pl.ANY),
            scratch_shapes=[pltpu.VMEM((EXPERT_BLOCK, d), F32), pltpu.VMEM((2, n_slots, d), F32),
                            pltpu.SemaphoreType.DMA((2,)), pltpu.SemaphoreType.DMA],
        ),
        out_shape=jax.ShapeDtypeStruct((n_pad, d), F32),
        compiler_params=_params(1),
        name="dispatch",
    )(pad_ends, counts, seg_rows, seg_src, seg_dst, slots, h2)


def _expert_kernel(be_ref, nu_ref, x_ref, wgu_ref, bgu_ref, wd_ref, bd_ref, y_ref, wgu_bf, wd_bf):
    d_exp = wd_ref.shape[1]
    i = pl.program_id(0)

    @pl.when(jnp.logical_or(i == 0, be_ref[i] != be_ref[jnp.maximum(i - 1, 0)]))
    def _():
        wgu_bf[...] = wgu_ref[0].astype(BF16)
        wd_bf[...] = wd_ref[0].astype(BF16)

    @pl.when(i < nu_ref[0])
    def _():
        gu = _dot(x_ref[...].astype(BF16), wgu_bf[...]) + bgu_ref[0]
        gate = jnp.minimum(gu[:, :d_exp], SWIGLU_LIMIT)
        up = jnp.clip(gu[:, d_exp:], -SWIGLU_LIMIT, SWIGLU_LIMIT)
        glu = gate * jax.nn.sigmoid(SWIGLU_ALPHA * gate)
        y_ref[...] = _dot(((up + 1.0) * glu).astype(BF16), wd_bf[...]) + bd_ref[0]

    @pl.when(i >= nu_ref[0])
    def _():
        y_ref[...] = jnp.zeros_like(y_ref)


def _experts(buf, block_e, n_used, w_gu, b_gu, w_down, b_down):
    n_pad, d = buf.shape
    d_exp = w_down.shape[1]
    used = lambda i, be, nu: jnp.minimum(i, nu[0] - 1)
    rows = pl.BlockSpec((EXPERT_BLOCK, d), lambda i, be, nu: (used(i, be, nu), 0))
    out_rows = pl.BlockSpec((EXPERT_BLOCK, d), lambda i, be, nu: (i, 0))
    per_e = lambda *shape: pl.BlockSpec((1,) + shape, lambda i, be, nu: (be[used(i, be, nu)], 0, 0))
    return pl.pallas_call(
        _expert_kernel,
        grid_spec=pltpu.PrefetchScalarGridSpec(
            num_scalar_prefetch=2,
            grid=(n_pad // EXPERT_BLOCK,),
            in_specs=[rows, per_e(d, 2 * d_exp), per_e(1, 2 * d_exp), per_e(d_exp, d), per_e(1, d)],
            out_specs=out_rows,
            scratch_shapes=[pltpu.VMEM((d, 2 * d_exp), BF16), pltpu.VMEM((d_exp, d), BF16)],
        ),
        out_shape=jax.ShapeDtypeStruct((n_pad, d), F32),
        compiler_params=_params(1),
        name="experts",
    )(block_e, n_used, buf, w_gu, b_gu, w_down, b_down)


def _combine_kernel(n8_ref, so_ref, go_ref, slot_ref, tw_ref, x1_ref, p_ref, gp_ref, wg_ref, wp_ref, gfin_ref, y_ref,
                    o_ref, rows_ref, sems, *, final_norm):
    t = x1_ref.shape[0]
    n_slots = rows_ref.shape[1]
    i = pl.program_id(0)
    cur = lax.rem(i, 2)

    def segment_copies(tile, into, act):
        def chunk(in_tile, in_buf, rows):
            act(pltpu.make_async_copy(y_ref.at[pl.ds(in_buf, rows)], rows_ref.at[into, pl.ds(in_tile, rows)],
                                      sems.at[into]))
        _for_segment_chunks(n8_ref, so_ref, go_ref, tile, chunk)

    @pl.when(i == 0)
    def _():
        rows_ref[...] = jnp.zeros_like(rows_ref)
        segment_copies(0, 0, lambda c: c.start())

    @pl.when(i + 1 < pl.num_programs(0))
    def _():
        segment_copies(i + 1, 1 - cur, lambda c: c.start())

    segment_copies(i, cur, lambda c: c.wait())

    slots = slot_ref[...]
    tw = tw_ref[...]
    slot_id = lax.broadcasted_iota(jnp.int32, (t, n_slots), 1)
    weights = jnp.zeros((t, n_slots), F32)
    for kk in range(TOP_K):
        weights = jnp.where(slot_id == slots[:, kk:kk + 1], tw[:, kk:kk + 1], weights)
    w_hi, w_lo = _split_bf16(weights)
    y_hi, y_lo = _split_bf16(rows_ref[cur])
    x2 = x1_ref[...] + (_dot(w_hi, y_hi) + _dot(w_lo, y_hi) + _dot(w_hi, y_lo))
    hp = _rms(x2, gp_ref[...]).astype(BF16)
    gate = jax.nn.sigmoid(_dot(hp, wg_ref[...]))
    x3 = x2 + gate * _dot(p_ref[...].astype(BF16), wp_ref[...])
    o_ref[...] = _rms(x3, gfin_ref[...]) if final_norm else x3


def _combine(slots, top_w, x1, p2d, y_buf, seg_rows, seg_src, seg_dst, g_ple, w_ple_gate, w_ple_proj, g_final,
             final_norm, t):
    n, d = x1.shape
    ple = p2d.shape[1]
    n_slots = TOP_K * t + N_EXPERTS * SEG_ALIGN
    const = lambda *shape: pl.BlockSpec(shape, lambda i, *_: (0,) * len(shape))
    tile = lambda w: pl.BlockSpec((t, w), lambda i, *_: (i, 0))
    return pl.pallas_call(
        functools.partial(_combine_kernel, final_norm=final_norm),
        grid_spec=pltpu.PrefetchScalarGridSpec(
            num_scalar_prefetch=3,
            grid=(n // t,),
            in_specs=[tile(LANES), tile(LANES), tile(d), tile(ple), const(1, d), const(d, d), const(ple, d),
                      const(1, d), pl.BlockSpec(memory_space=pl.ANY)],
            out_specs=tile(d),
            scratch_shapes=[pltpu.VMEM((2, n_slots, d), F32), pltpu.SemaphoreType.DMA((2,))],
        ),
        out_shape=jax.ShapeDtypeStruct((n, d), F32),
        compiler_params=_params(1),
        name="combine",
    )(seg_rows, seg_src, seg_dst, slots, top_w, x1, p2d, g_ple, w_ple_gate, w_ple_proj, g_final, y_buf)


def _layer(x, p, g_mix, w_in, w_pool_grp, pool_scale, w_pool_up, w_attn_up, w_out, g_ffn, w_router, b_router,
           w_gu, b_gu, w_down, b_down, g_ple, w_ple_gate, w_ple_proj, g_final, final_norm):
    b, s, d = x.shape
    n = b * s
    pool_w = len(POOL_WINDOWS) * POOL_CH
    att_w = N_HEADS * HEAD_DIM
    seq_tile = min(512, s)
    tok_tile = min(512, n)
    sort_tile = min(256, n)
    attn_blk = min(256, s)

    row = lambda v: v.reshape(1, -1)
    w_in = w_in.astype(BF16)
    w_u, w_qkv = w_in[:, :pool_w], w_in[:, pool_w:pool_w + 3 * att_w]
    w_gp, w_ga = w_in[:, pool_w + 3 * att_w:pool_w + 3 * att_w + d], w_in[:, pool_w + 3 * att_w + d:]
    q, k, v, pool_branch, g_att = _mix_in(x, row(g_mix), w_u, w_qkv, w_gp, w_ga, w_pool_grp.astype(BF16),
                                          row(pool_scale), w_pool_up.astype(BF16), seq_tile)
    y_att = _attention(q, k, v, attn_blk)

    wr_hi = w_router.astype(BF16)
    wr_lo = (w_router - wr_hi.astype(F32)).astype(BF16)
    x1, h2, e_out, r_out, w_top, tile_counts = _mix_out(
        x.reshape(n, d), y_att.reshape(n, att_w), pool_branch.reshape(n, d), g_att.reshape(n, d),
        w_attn_up.astype(BF16), w_out.astype(BF16), row(g_ffn), wr_hi, wr_lo, row(b_router), tok_tile, sort_tile)

    n_tiles = n // sort_tile
    seg_rows = -(-tile_counts[:, 0, :].astype(jnp.int32) // SEG_ALIGN) * SEG_ALIGN
    seg_src = jnp.cumsum(seg_rows, axis=1) - seg_rows
    group_rows = jnp.sum(seg_rows, axis=0)
    padded = (group_rows + EXPERT_BLOCK - 1) // EXPERT_BLOCK * EXPERT_BLOCK
    pad_ends = jnp.cumsum(padded)
    seg_dst = (pad_ends - padded)[None, :] + jnp.cumsum(seg_rows, axis=0) - seg_rows
    tile_of = (jnp.arange(n, dtype=jnp.int32) // sort_tile)[:, None]
    top_e, rank = e_out[:, :TOP_K], r_out[:, :TOP_K]
    slots = jnp.pad(seg_src[tile_of, top_e] + rank, ((0, 0), (0, LANES - TOP_K)))
    n_blocks = -(-(n * TOP_K + n_tiles * N_EXPERTS * (SEG_ALIGN - 1)) // EXPERT_BLOCK) + N_EXPERTS
    block_starts = jnp.arange(n_blocks, dtype=jnp.int32) * EXPERT_BLOCK
    block_e = jnp.minimum(jnp.sum((pad_ends[None, :] <= block_starts[:, None]).astype(jnp.int32), axis=1),
                          N_EXPERTS - 1)
    n_used = (pad_ends[-1:] // EXPERT_BLOCK).astype(jnp.int32)
    seg = (seg_rows.reshape(-1), seg_src.reshape(-1), seg_dst.reshape(-1))

    buf = _dispatch(h2, slots, pad_ends, group_rows, *seg, n_blocks * EXPERT_BLOCK, sort_tile)
    y_buf = _experts(buf, block_e, n_used, w_gu, b_gu[:, None, :], w_down, b_down[:, None, :])
    out = _combine(slots, w_top, x1, p.reshape(n, -1), y_buf, *seg, row(g_ple), w_ple_gate.astype(BF16),
                   w_ple_proj.astype(BF16), row(g_final), final_norm, sort_tile)
    return out.reshape(b, s, d)


def kernel(x, p, g_mix, w_in, w_pool_grp, pool_scale, w_pool_up, w_attn_up, w_out, g_ffn, w_router, b_router,
           w_gu, b_gu, w_down, b_down, g_ple, w_ple_gate, w_ple_proj, g_final):
    depth = p.shape[0]
    for i in range(depth):
        x = _layer(x, p[i], g_mix[i], w_in[i], w_pool_grp[i], pool_scale[i], w_pool_up[i], w_attn_up[i], w_out[i],
                   g_ffn[i], w_router[i], b_router[i], w_gu[i], b_gu[i], w_down[i], b_down[i], g_ple[i],
                   w_ple_gate[i], w_ple_proj[i], g_final, i == depth - 1)
    return x
```

```python
import functools

import jax
import jax.numpy as jnp
from jax import lax
from jax.experimental import pallas as pl
from jax.experimental.pallas import tpu as pltpu

F32 = jnp.float32
BF16 = jnp.bfloat16

N_HEADS = 8
HEAD_DIM = 64
HEADS_PER_STEP = 2
ROW_CHUNK = 128
POOL_WINDOWS = (2, 4, 8, 16)
POOL_CH = 128
POOL_HIST = 16
N_EXPERTS = 32
TOP_K = 4
EXPERT_BLOCK = 512
SEG_ALIGN = 8
BIG_CHUNK = 32
WAIT_CHUNK = 256
SWIGLU_LIMIT = 7.0
SWIGLU_ALPHA = 1.702
EPS = 1e-6
LOG2_E = 1.4426950408889634
UNDERFLOW_BITS = 160.0
LANES = 128
VMEM_LIMIT = 56 * 1024 * 1024


def _params(n_axes, flags=None):
    return pltpu.CompilerParams(dimension_semantics=("arbitrary",) * n_axes, vmem_limit_bytes=VMEM_LIMIT,
                                flags=flags)


def _rms(xf, g):
    return xf * lax.rsqrt(jnp.mean(xf * xf, axis=-1, keepdims=True) + EPS) * g


def _dot(a, b):
    return jnp.dot(a, b, preferred_element_type=F32)


def _split_bf16(a):
    hi = a.astype(BF16)
    lo = (a - hi.astype(F32)).astype(BF16)
    return hi, lo


def _mix_in_kernel(x_ref, g_ref, wu_ref, wqkv_ref, wgp_ref, wga_ref, wgrp_ref, pscale_ref, wpu_ref,
                   q_ref, k_ref, v_ref, pb_ref, ga_ref, hist_ref):
    s = pl.program_id(1)
    ts = x_ref.shape[1]
    att_w = q_ref.shape[2]

    @pl.when(s == 0)
    def _():
        hist_ref[...] = jnp.zeros_like(hist_ref)

    h = _rms(x_ref[0], g_ref[...]).astype(BF16)
    u = _dot(h, wu_ref[...])
    ext = jnp.concatenate([hist_ref[...], u], axis=0)
    hist_ref[...] = u[ts - POOL_HIST:, :]
    t_pos = s * ts + lax.broadcasted_iota(jnp.int32, (ts, 1), 0)
    y_groups = []
    for g, w in enumerate(POOL_WINDOWS):
        a = ext[:, g * POOL_CH:(g + 1) * POOL_CH]
        sh = 1
        while sh < w:
            a = a + pltpu.roll(a, sh, axis=0)
            sh *= 2
        count = jnp.minimum(t_pos + 1, w).astype(F32)
        pooled = a[POOL_HIST:, :] / count - u[:, g * POOL_CH:(g + 1) * POOL_CH]
        y_groups.append(_dot(pooled.astype(BF16), wgrp_ref[g]))
    y_pool = jnp.concatenate(y_groups, axis=1) * pscale_ref[...]
    pool_up = _dot(y_pool.astype(BF16), wpu_ref[...])
    g_pool = jax.nn.sigmoid(_dot(h, wgp_ref[...]))
    pb_ref[0] = (g_pool * pool_up).astype(BF16)
    ga_ref[0] = jax.nn.sigmoid(_dot(h, wga_ref[...])).astype(BF16)
    qkv = _dot(h, wqkv_ref[...])
    q_ref[0] = (qkv[:, :att_w] * (HEAD_DIM ** -0.5 * LOG2_E)).astype(BF16)
    k_ref[0] = qkv[:, att_w:2 * att_w].astype(BF16)
    v_ref[0] = qkv[:, 2 * att_w:].astype(BF16)


def _mix_in(x, g_mix, w_u, w_qkv, w_gp, w_ga, w_grp, pool_scale, w_pool_up, ts):
    b, s, d = x.shape
    pool_w = w_u.shape[1]
    att_w = w_qkv.shape[1] // 3
    const = lambda *shape: pl.BlockSpec(shape, lambda i, j: (0,) * len(shape))
    tile = lambda w: pl.BlockSpec((1, ts, w), lambda i, j: (i, j, 0))
    return pl.pallas_call(
        _mix_in_kernel,
        grid=(b, s // ts),
        in_specs=[tile(d), const(1, d), const(d, pool_w), const(d, 3 * att_w), const(d, d), const(d, d),
                  const(len(POOL_WINDOWS), POOL_CH, POOL_CH), const(1, pool_w), const(pool_w, d)],
        out_specs=[tile(att_w), tile(att_w), tile(att_w), tile(d), tile(d)],
        out_shape=[jax.ShapeDtypeStruct((b, s, att_w), BF16)] * 3 + [jax.ShapeDtypeStruct((b, s, d), BF16)] * 2,
        scratch_shapes=[pltpu.VMEM((POOL_HIST, pool_w), F32)],
        compiler_params=_params(2),
        name="mix_in",
    )(x, g_mix, w_u, w_qkv, w_gp, w_ga, w_grp, pool_scale, w_pool_up)


def _attn_kernel(q_ref, k_ref, v_ref, tri_ref, o_ref, *, blk):
    i = pl.program_id(2)
    q2 = q_ref[0]
    lane = lax.broadcasted_iota(jnp.int32, (blk, LANES), 1)
    row = lax.broadcasted_iota(jnp.int32, (blk, blk), 0)
    col = lax.broadcasted_iota(jnp.int32, (blk, blk), 1)
    causal = col < row
    heads = (lane < HEAD_DIM, lane >= HEAD_DIM)
    qs = [jnp.where(hm, q2, jnp.zeros_like(q2)) for hm in heads]

    def keys_values(j):
        start = pl.multiple_of(j * blk, blk)
        return k_ref[0, pl.ds(start, blk), :], v_ref[0, pl.ds(start, blk), :]

    def scores(qh, kj, diagonal):
        z = lax.dot_general(qh, kj, (((1,), (1,)), ((), ())), preferred_element_type=F32)
        softplus = jnp.maximum(z, 0.0) + jnp.log2(1.0 + jnp.exp2(-jnp.abs(z)))
        if diagonal:
            softplus = jnp.where(causal, softplus, 0.0)
        hi, lo = _split_bf16(softplus)
        return z, _dot(jnp.concatenate([hi, lo], axis=1), tri_ref[...])

    def accumulate(z, incl, acc, run, vj, diagonal):
        w = jnp.exp2(z - incl - jnp.concatenate([run] * (blk // LANES), axis=1))
        if diagonal:
            w = jnp.where(causal, w, 0.0)
        return acc + _dot(w.astype(BF16), vj), run + jnp.broadcast_to(incl[:, 0:1], (blk, LANES))

    has_prev = i > 0
    kd, vd = keys_values(i)
    kp, vp = keys_values(jnp.maximum(i - 1, 0))
    front_d = [scores(qh, kd, True) for qh in qs]
    front_p = [scores(qh, kp, False) for qh in qs]
    zero = jnp.zeros((blk, LANES), F32)
    accs, runs = [], []
    for (zd, incl_d), (zp, incl_p) in zip(front_d, front_p):
        acc_d, run_d = accumulate(zd, incl_d, zero, zero, vd, True)
        acc_p, run_p = accumulate(zp, incl_p, acc_d, run_d, vp, False)
        accs.append(jnp.where(has_prev, acc_p, acc_d))
        runs.append(jnp.where(has_prev, run_p, run_d))

    def live(c):
        n, _, _, min_run = c
        return jnp.logical_and(n < i, min_run < UNDERFLOW_BITS)

    def step(c):
        n, accs, runs, _ = c
        kj, vj = keys_values(i - 1 - n)
        front = [scores(qh, kj, False) for qh in qs]
        back = [accumulate(z, incl, acc, run, vj, False) for (z, incl), acc, run in zip(front, accs, runs)]
        accs, runs = [a for a, _ in back], [r for _, r in back]
        return n + 1, accs, runs, jnp.minimum(jnp.min(runs[0]), jnp.min(runs[1]))

    min_run = jnp.minimum(jnp.min(runs[0]), jnp.min(runs[1]))
    _, accs, _, _ = lax.while_loop(live, step, (jnp.int32(1), accs, runs, min_run))
    o_ref[0] = jnp.where(heads[0], accs[0], accs[1]).astype(o_ref.dtype)


def _attention(q, k, v, blk):
    b, s, att_w = q.shape
    tri = (jnp.arange(blk)[:, None] >= jnp.arange(blk)[None, :]).astype(BF16)
    tri2 = jnp.concatenate([tri, tri], axis=0)
    qo_spec = pl.BlockSpec((1, blk, LANES), lambda bi, hp, i: (bi, i, hp))
    kv_spec = pl.BlockSpec((1, s, LANES), lambda bi, hp, i: (bi, 0, hp))
    return pl.pallas_call(
        functools.partial(_attn_kernel, blk=blk),
        grid=(b, att_w // LANES, s // blk),
        in_specs=[qo_spec, kv_spec, kv_spec, pl.BlockSpec((2 * blk, blk), lambda bi, hp, i: (0, 0))],
        out_specs=qo_spec,
        out_shape=jax.ShapeDtypeStruct((b, s, att_w), BF16),
        compiler_params=_params(3),
        name="attn",
    )(q, k, v, tri2)


def _mix_out_kernel(x_ref, ya_ref, pb_ref, ga_ref, wau_ref, wo_ref, gf_ref, wrh_ref, wrl_ref, br_ref, lt_ref, up_ref,
                    x1_ref, h2_ref, s_ref, w_ref, cnt_ref):
    t = x_ref.shape[0]

    att_up = _dot(ya_ref[...], wau_ref[...])
    merged = pb_ref[...].astype(F32) + ga_ref[...].astype(F32) * att_up
    x1 = x_ref[...] + _dot(merged.astype(BF16), wo_ref[...])
    x1_ref[...] = x1
    h2 = _rms(x1, gf_ref[...])
    h2_ref[...] = h2.astype(h2_ref.dtype)

    hi, lo = _split_bf16(h2)
    logits = _dot(hi, wrh_ref[...]) + _dot(lo, wrh_ref[...]) + _dot(hi, wrl_ref[...]) + br_ref[...]
    lane_e = lax.broadcasted_iota(jnp.int32, (t, N_EXPERTS), 1)
    vals, idxs = [], []
    for _ in range(TOP_K):
        m = jnp.max(logits, axis=-1, keepdims=True)
        idx = jnp.min(jnp.where(logits == m, lane_e, N_EXPERTS), axis=-1, keepdims=True)
        vals.append(m)
        idxs.append(idx)
        logits = jnp.where(lane_e == idx, -jnp.inf, logits)
    exps = [jnp.exp(m - vals[0]) for m in vals]
    denom = exps[0] + exps[1] + exps[2] + exps[3]

    onehot = jnp.zeros((t, N_EXPERTS), F32)
    for idx in idxs:
        onehot = onehot + jnp.where(lane_e == idx, 1.0, 0.0)
    pos = _dot(lt_ref[...], onehot.astype(BF16))
    sort_tile = t // cnt_ref.shape[0]
    seg_starts = []
    for j in range(cnt_ref.shape[0]):
        cnt = jnp.sum(onehot[j * sort_tile:(j + 1) * sort_tile], axis=0, keepdims=True)
        cnt_ref[j] = cnt
        seg_rows = jnp.ceil(cnt * (1.0 / SEG_ALIGN)) * SEG_ALIGN
        before = _dot(jnp.broadcast_to(seg_rows, (SEG_ALIGN, N_EXPERTS)).astype(BF16), up_ref[...])[0:1]
        seg_starts.append(jnp.broadcast_to(before, (sort_tile, N_EXPERTS)))
    pos = pos + jnp.concatenate(seg_starts, axis=0)

    lane = lax.broadcasted_iota(jnp.int32, (t, LANES), 1)
    s_out = jnp.zeros((t, LANES), jnp.int32)
    w_out = jnp.zeros((t, LANES), F32)
    for kk in range(TOP_K):
        slot = jnp.sum(jnp.where(lane_e == idxs[kk], pos, 0.0), axis=-1, keepdims=True).astype(jnp.int32)
        s_out = jnp.where(lane == kk, slot, s_out)
        w_out = jnp.where(lane == kk, exps[kk] / denom, w_out)
    s_ref[...] = s_out
    w_ref[...] = w_out


def _mix_out(x2d, y_att, pool_branch, g_att, w_attn_up, w_out, g_ffn, wr_hi, wr_lo, b_router, t, sort_tile):
    n, d = x2d.shape
    att_w = y_att.shape[1]
    r, c = jnp.arange(t)[:, None], jnp.arange(t)[None, :]
    lower = jnp.logical_and(r > c, r // sort_tile == c // sort_tile).astype(BF16)
    experts = jnp.arange(N_EXPERTS)
    before = (experts[:, None] < experts[None, :]).astype(BF16)
    const = lambda *shape: pl.BlockSpec(shape, lambda i: (0,) * len(shape))
    tile = lambda w: pl.BlockSpec((t, w), lambda i: (i, 0))
    return pl.pallas_call(
        _mix_out_kernel,
        grid=(n // t,),
        in_specs=[tile(d), tile(att_w), tile(d), tile(d), const(att_w, d), const(d, d), const(1, d),
                  const(d, N_EXPERTS), const(d, N_EXPERTS), const(1, N_EXPERTS), const(t, t),
                  const(N_EXPERTS, N_EXPERTS)],
        out_specs=[tile(d), tile(d), tile(LANES), tile(LANES),
                   pl.BlockSpec((t // sort_tile, 1, N_EXPERTS), lambda i: (i, 0, 0))],
        out_shape=[jax.ShapeDtypeStruct((n, d), F32), jax.ShapeDtypeStruct((n, d), BF16),
                   jax.ShapeDtypeStruct((n, LANES), jnp.int32), jax.ShapeDtypeStruct((n, LANES), F32),
                   jax.ShapeDtypeStruct((n // sort_tile, 1, N_EXPERTS), F32)],
        compiler_params=_params(1),
        name="mix_out",
    )(x2d, y_att, pool_branch, g_att, w_attn_up, w_out, g_ffn, wr_hi, wr_lo, b_router, lower, before)


def _for_segment_chunks(n8_ref, so_ref, go_ref, tile, act):
    def segment(e, c):
        n8 = n8_ref[tile * N_EXPERTS + e]
        in_tile = so_ref[tile * N_EXPERTS + e]
        in_buf = go_ref[tile * N_EXPERTS + e]
        n_big = lax.shift_right_logical(n8, BIG_CHUNK.bit_length() - 1)
        done = n_big * BIG_CHUNK

        def big(j, c2):
            act(pl.multiple_of(in_tile + j * BIG_CHUNK, SEG_ALIGN), pl.multiple_of(in_buf + j * BIG_CHUNK, SEG_ALIGN),
                BIG_CHUNK)
            return c2

        def small(j, c2):
            act(pl.multiple_of(in_tile + done + j * SEG_ALIGN, SEG_ALIGN),
                pl.multiple_of(in_buf + done + j * SEG_ALIGN, SEG_ALIGN), SEG_ALIGN)
            return c2

        lax.fori_loop(0, n_big, big, 0)
        lax.fori_loop(0, lax.shift_right_logical(n8 - done, SEG_ALIGN.bit_length() - 1), small, 0)
        return c

    lax.fori_loop(0, N_EXPERTS, segment, 0)


def _wait_rows(total, wait_chunk):
    n_big = lax.shift_right_logical(total, WAIT_CHUNK.bit_length() - 1)

    def big(j, c):
        wait_chunk(WAIT_CHUNK)
        return c

    lax.fori_loop(0, n_big, big, 0)
    rest = total - n_big * WAIT_CHUNK
    size = WAIT_CHUNK // 2
    while size >= SEG_ALIGN:
        @pl.when(jnp.bitwise_and(rest, size) != 0)
        def _(size=size):
            wait_chunk(size)
        size //= 2


def _dispatch_kernel(pe_ref, cnt_ref, n8_ref, so_ref, go_ref, tot_ref, slot_ref, h_ref, buf_ref, zero_ref, sorted_ref,
                     sems, zsem):
    t = h_ref.shape[0]
    n_slots = sorted_ref.shape[1]
    n_blocks = buf_ref.shape[0] // EXPERT_BLOCK
    i = pl.program_id(0)
    cur = lax.rem(i, 2)

    def copy(into, in_tile, in_buf, rows):
        return pltpu.make_async_copy(sorted_ref.at[into, pl.ds(in_tile, rows)], buf_ref.at[pl.ds(in_buf, rows)],
                                     sems.at[into])

    def wait_tile(tile, into):
        _wait_rows(tot_ref[tile], lambda rows: copy(into, 0, 0, rows).wait())

    @pl.when(i == 0)
    def _():
        zero_ref[...] = jnp.zeros_like(zero_ref)

        def zero_copy(start):
            start = pl.multiple_of(start, EXPERT_BLOCK)
            return pltpu.make_async_copy(zero_ref, buf_ref.at[pl.ds(start, EXPERT_BLOCK)], zsem)

        def start_last(e, c):
            @pl.when(cnt_ref[e] > 0)
            def _():
                zero_copy(pe_ref[e] - EXPERT_BLOCK).start()
            return c

        def wait_last(e, c):
            @pl.when(cnt_ref[e] > 0)
            def _():
                zero_copy(pe_ref[e] - EXPERT_BLOCK).wait()
            return c

        def start_tail(b, c):
            zero_copy(b * EXPERT_BLOCK).start()
            return c

        def wait_tail(b, c):
            zero_copy(b * EXPERT_BLOCK).wait()
            return c

        n_used = pe_ref[N_EXPERTS - 1] // EXPERT_BLOCK
        lax.fori_loop(0, N_EXPERTS, start_last, 0)
        lax.fori_loop(n_used, n_blocks, start_tail, 0)
        lax.fori_loop(0, N_EXPERTS, wait_last, 0)
        lax.fori_loop(n_used, n_blocks, wait_tail, 0)

    @pl.when(i >= 2)
    def _():
        wait_tile(i - 2, cur)

    slots = slot_ref[...]
    slot_id = lax.broadcasted_iota(jnp.int32, (t, n_slots), 1)
    hit = slot_id == slots[:, 0:1]
    for kk in range(1, TOP_K):
        hit = jnp.logical_or(hit, slot_id == slots[:, kk:kk + 1])
    select = jnp.where(hit, 1.0, 0.0).astype(BF16)
    sorted_ref[cur] = lax.dot_general(select, h_ref[...], (((0,), (0,)), ((), ())), preferred_element_type=F32)
    _for_segment_chunks(n8_ref, so_ref, go_ref, i,
                        lambda in_tile, in_buf, rows: copy(cur, in_tile, in_buf, rows).start())

    @pl.when(i == pl.num_programs(0) - 1)
    def _():
        @pl.when(i >= 1)
        def _():
            wait_tile(i - 1, 1 - cur)
        wait_tile(i, cur)


def _dispatch(h2, slots, pad_ends, counts, seg_rows, seg_src, seg_dst, tile_rows, n_pad, t):
    n, d = h2.shape
    n_slots = TOP_K * t + N_EXPERTS * SEG_ALIGN
    return pl.pallas_call(
        _dispatch_kernel,
        grid_spec=pltpu.PrefetchScalarGridSpec(
            num_scalar_prefetch=6,
            grid=(n // t,),
            in_specs=[pl.BlockSpec((t, LANES), lambda i, *_: (i, 0)), pl.BlockSpec((t, d), lambda i, *_: (i, 0))],
            out_specs=pl.BlockSpec(memory_space=pl.ANY),
            scratch_shapes=[pltpu.VMEM((EXPERT_BLOCK, d), F32), pltpu.VMEM((2, n_slots, d), F32),
                            pltpu.SemaphoreType.DMA((2,)), pltpu.SemaphoreType.DMA],
        ),
        out_shape=jax.ShapeDtypeStruct((n_pad, d), F32),
        compiler_params=_params(1),
        name="dispatch",
    )(pad_ends, counts, seg_rows, seg_src, seg_dst, tile_rows, slots, h2)


def _expert_kernel(be_ref, nu_ref, x_ref, wgu_ref, bgu_ref, wd_ref, bd_ref, y_ref, wgu_bf, wd_bf):
    d_exp = wd_ref.shape[1]
    i = pl.program_id(0)

    @pl.when(jnp.logical_or(i == 0, be_ref[i] != be_ref[jnp.maximum(i - 1, 0)]))
    def _():
        wgu_bf[...] = wgu_ref[0].astype(BF16)
        wd_bf[...] = wd_ref[0].astype(BF16)

    @pl.when(i < nu_ref[0])
    def _():
        gu = _dot(x_ref[...].astype(BF16), wgu_bf[...]) + bgu_ref[0]
        gate = jnp.minimum(gu[:, :d_exp], SWIGLU_LIMIT)
        up = jnp.clip(gu[:, d_exp:], -SWIGLU_LIMIT, SWIGLU_LIMIT)
        glu = gate * jax.nn.sigmoid(SWIGLU_ALPHA * gate)
        y_ref[...] = _dot(((up + 1.0) * glu).astype(BF16), wd_bf[...]) + bd_ref[0]

    @pl.when(i >= nu_ref[0])
    def _():
        y_ref[...] = jnp.zeros_like(y_ref)


def _experts(buf, block_e, n_used, w_gu, b_gu, w_down, b_down):
    n_pad, d = buf.shape
    d_exp = w_down.shape[1]
    used = lambda i, be, nu: jnp.minimum(i, nu[0] - 1)
    rows = pl.BlockSpec((EXPERT_BLOCK, d), lambda i, be, nu: (used(i, be, nu), 0))
    out_rows = pl.BlockSpec((EXPERT_BLOCK, d), lambda i, be, nu: (i, 0))
    per_e = lambda *shape: pl.BlockSpec((1,) + shape, lambda i, be, nu: (be[used(i, be, nu)], 0, 0))
    return pl.pallas_call(
        _expert_kernel,
        grid_spec=pltpu.PrefetchScalarGridSpec(
            num_scalar_prefetch=2,
            grid=(n_pad // EXPERT_BLOCK,),
            in_specs=[rows, per_e(d, 2 * d_exp), per_e(1, 2 * d_exp), per_e(d_exp, d), per_e(1, d)],
            out_specs=out_rows,
            scratch_shapes=[pltpu.VMEM((d, 2 * d_exp), BF16), pltpu.VMEM((d_exp, d), BF16)],
        ),
        out_shape=jax.ShapeDtypeStruct((n_pad, d), F32),
        compiler_params=_params(1),
        name="experts",
    )(block_e, n_used, buf, w_gu, b_gu, w_down, b_down)


def _combine_kernel(n8_ref, so_ref, go_ref, tot_ref, slot_ref, tw_ref, x1_ref, p_ref, gp_ref, wg_ref, wp_ref, gfin_ref,
                    y_ref, o_ref, rows_ref, sems, *, final_norm):
    t = x1_ref.shape[0]
    n_slots = rows_ref.shape[1]
    i = pl.program_id(0)
    cur = lax.rem(i, 2)

    def copy(into, in_tile, in_buf, rows):
        return pltpu.make_async_copy(y_ref.at[pl.ds(in_buf, rows)], rows_ref.at[into, pl.ds(in_tile, rows)],
                                     sems.at[into])

    def start_tile(tile, into):
        _for_segment_chunks(n8_ref, so_ref, go_ref, tile,
                            lambda in_tile, in_buf, rows: copy(into, in_tile, in_buf, rows).start())

    @pl.when(i == 0)
    def _():
        rows_ref[...] = jnp.zeros_like(rows_ref)
        start_tile(0, 0)

    @pl.when(i + 1 < pl.num_programs(0))
    def _():
        start_tile(i + 1, 1 - cur)

    _wait_rows(tot_ref[i], lambda rows: copy(cur, 0, 0, rows).wait())

    slots = slot_ref[...]
    tw = tw_ref[...]
    slot_id = lax.broadcasted_iota(jnp.int32, (t, n_slots), 1)
    weights = jnp.zeros((t, n_slots), F32)
    for kk in range(TOP_K):
        weights = jnp.where(slot_id == slots[:, kk:kk + 1], tw[:, kk:kk + 1], weights)
    w_hi, w_lo = _split_bf16(weights)
    y_hi, y_lo = _split_bf16(rows_ref[cur])
    x2 = x1_ref[...] + (_dot(w_hi, y_hi) + _dot(w_lo, y_hi) + _dot(w_hi, y_lo))
    hp = _rms(x2, gp_ref[...]).astype(BF16)
    gate = jax.nn.sigmoid(_dot(hp, wg_ref[...]))
    x3 = x2 + gate * _dot(p_ref[...].astype(BF16), wp_ref[...])
    o_ref[...] = _rms(x3, gfin_ref[...]) if final_norm else x3


def _combine(slots, top_w, x1, p2d, y_buf, seg_rows, seg_src, seg_dst, tile_rows, g_ple, w_ple_gate, w_ple_proj,
             g_final, final_norm, t):
    n, d = x1.shape
    ple = p2d.shape[1]
    n_slots = TOP_K * t + N_EXPERTS * SEG_ALIGN
    const = lambda *shape: pl.BlockSpec(shape, lambda i, *_: (0,) * len(shape))
    tile = lambda w: pl.BlockSpec((t, w), lambda i, *_: (i, 0))
    return pl.pallas_call(
        functools.partial(_combine_kernel, final_norm=final_norm),
        grid_spec=pltpu.PrefetchScalarGridSpec(
            num_scalar_prefetch=4,
            grid=(n // t,),
            in_specs=[tile(LANES), tile(LANES), tile(d), tile(ple), const(1, d), const(d, d), const(ple, d),
                      const(1, d), pl.BlockSpec(memory_space=pl.ANY)],
            out_specs=tile(d),
            scratch_shapes=[pltpu.VMEM((2, n_slots, d), F32), pltpu.SemaphoreType.DMA((2,))],
        ),
        out_shape=jax.ShapeDtypeStruct((n, d), F32),
        compiler_params=_params(1),
        name="combine",
    )(seg_rows, seg_src, seg_dst, tile_rows, slots, top_w, x1, p2d, g_ple, w_ple_gate, w_ple_proj, g_final, y_buf)


def _layer(x, p, g_mix, w_in, w_pool_grp, pool_scale, w_pool_up, w_attn_up, w_out, g_ffn, w_router, b_router,
           w_gu, b_gu, w_down, b_down, g_ple, w_ple_gate, w_ple_proj, g_final, final_norm):
    b, s, d = x.shape
    n = b * s
    pool_w = len(POOL_WINDOWS) * POOL_CH
    att_w = N_HEADS * HEAD_DIM
    seq_tile = min(512, s)
    tok_tile = min(512, n)
    sort_tile = min(256, n)
    attn_blk = min(256, s)

    row = lambda v: v.reshape(1, -1)
    w_in = w_in.astype(BF16)
    w_u, w_qkv = w_in[:, :pool_w], w_in[:, pool_w:pool_w + 3 * att_w]
    w_gp, w_ga = w_in[:, pool_w + 3 * att_w:pool_w + 3 * att_w + d], w_in[:, pool_w + 3 * att_w + d:]
    q, k, v, pool_branch, g_att = _mix_in(x, row(g_mix), w_u, w_qkv, w_gp, w_ga, w_pool_grp.astype(BF16),
                                          row(pool_scale), w_pool_up.astype(BF16), seq_tile)
    y_att = _attention(q, k, v, attn_blk)

    wr_hi = w_router.astype(BF16)
    wr_lo = (w_router - wr_hi.astype(F32)).astype(BF16)
    x1, h2, slots, w_top, tile_counts = _mix_out(
        x.reshape(n, d), y_att.reshape(n, att_w), pool_branch.reshape(n, d), g_att.reshape(n, d),
        w_attn_up.astype(BF16), w_out.astype(BF16), row(g_ffn), wr_hi, wr_lo, row(b_router), tok_tile, sort_tile)

    n_tiles = n // sort_tile
    seg_rows = -(-tile_counts[:, 0, :].astype(jnp.int32) // SEG_ALIGN) * SEG_ALIGN
    seg_src = jnp.cumsum(seg_rows, axis=1) - seg_rows
    group_rows = jnp.sum(seg_rows, axis=0)
    padded = (group_rows + EXPERT_BLOCK - 1) // EXPERT_BLOCK * EXPERT_BLOCK
    pad_ends = jnp.cumsum(padded)
    seg_dst = (pad_ends - padded)[None, :] + jnp.cumsum(seg_rows, axis=0) - seg_rows
    n_blocks = -(-(n * TOP_K + n_tiles * N_EXPERTS * (SEG_ALIGN - 1)) // EXPERT_BLOCK) + N_EXPERTS
    block_starts = jnp.arange(n_blocks, dtype=jnp.int32) * EXPERT_BLOCK
    block_e = jnp.minimum(jnp.sum((pad_ends[None, :] <= block_starts[:, None]).astype(jnp.int32), axis=1),
                          N_EXPERTS - 1)
    n_used = (pad_ends[-1:] // EXPERT_BLOCK).astype(jnp.int32)
    seg = (seg_rows.reshape(-1), seg_src.reshape(-1), seg_dst.reshape(-1), jnp.sum(seg_rows, axis=1))

    buf = _dispatch(h2, slots, pad_ends, group_rows, *seg, n_blocks * EXPERT_BLOCK, sort_tile)
    y_buf = _experts(buf, block_e, n_used, w_gu, b_gu[:, None, :], w_down, b_down[:, None, :])
    out = _combine(slots, w_top, x1, p.reshape(n, -1), y_buf, *seg, row(g_ple), w_ple_gate.astype(BF16),
                   w_ple_proj.astype(BF16), row(g_final), final_norm, sort_tile)
    return out.reshape(b, s, d)


def kernel(x, p, g_mix, w_in, w_pool_grp, pool_scale, w_pool_up, w_attn_up, w_out, g_ffn, w_router, b_router,
           w_gu, b_gu, w_down, b_down, g_ple, w_ple_gate, w_ple_proj, g_final):
    depth = p.shape[0]
    for i in range(depth):
        x = _layer(x, p[i], g_mix[i], w_in[i], w_pool_grp[i], pool_scale[i], w_pool_up[i], w_attn_up[i], w_out[i],
                   g_ffn[i], w_router[i], b_router[i], w_gu[i], b_gu[i], w_down[i], b_down[i], g_ple[i],
                   w_ple_gate[i], w_ple_proj[i], g_final, i == depth - 1)
    return x
```

```python
import functools

import jax
import jax.numpy as jnp
from jax import lax
from jax.experimental import pallas as pl
from jax.experimental.pallas import tpu as pltpu

F32 = jnp.float32
BF16 = jnp.bfloat16

N_HEADS = 8
HEAD_DIM = 64
POOL_WINDOWS = (2, 4, 8, 16)
POOL_CH = 128
POOL_HIST = 16
N_EXPERTS = 32
TOP_K = 4
EXPERT_BLOCK = 512
SEG_ALIGN = 8
BIG_CHUNK = 32
WAIT_CHUNK = 256
SWIGLU_LIMIT = 7.0
SWIGLU_ALPHA = 1.702
EPS = 1e-6
LOG2_E = 1.4426950408889634
UNDERFLOW_BITS = 160.0
LANES = 128
ROW_TILE = 512
SORT_TILE = 256
ATTN_BLOCK = 256
VMEM_LIMIT = 56 * 1024 * 1024


def _params(n_axes, flags=None):
    return pltpu.CompilerParams(dimension_semantics=("arbitrary",) * n_axes, vmem_limit_bytes=VMEM_LIMIT,
                                flags=flags)


def _rms(xf, g):
    return xf * lax.rsqrt(jnp.mean(xf * xf, axis=-1, keepdims=True) + EPS) * g


def _dot(a, b):
    return jnp.dot(a, b, preferred_element_type=F32)


def _split_bf16(a):
    hi = a.astype(BF16)
    lo = (a - hi.astype(F32)).astype(BF16)
    return hi, lo


def _mix_in_kernel(x_ref, g_ref, wu_ref, wqkv_ref, wgp_ref, wga_ref, wgrp_ref, pscale_ref, wpu_ref,
                   q_ref, k_ref, v_ref, pb_ref, ga_ref, hist_ref):
    s = pl.program_id(1)
    ts = x_ref.shape[1]
    att_w = q_ref.shape[2]

    @pl.when(s == 0)
    def _():
        hist_ref[...] = jnp.zeros_like(hist_ref)

    h = _rms(x_ref[0], g_ref[...]).astype(BF16)
    u = _dot(h, wu_ref[...])
    ext = jnp.concatenate([hist_ref[...], u], axis=0)
    hist_ref[...] = u[ts - POOL_HIST:, :]
    t_pos = s * ts + lax.broadcasted_iota(jnp.int32, (ts, 1), 0)
    y_groups = []
    for g, w in enumerate(POOL_WINDOWS):
        a = ext[:, g * POOL_CH:(g + 1) * POOL_CH]
        sh = 1
        while sh < w:
            a = a + pltpu.roll(a, sh, axis=0)
            sh *= 2
        count = jnp.minimum(t_pos + 1, w).astype(F32)
        pooled = a[POOL_HIST:, :] / count - u[:, g * POOL_CH:(g + 1) * POOL_CH]
        y_groups.append(_dot(pooled.astype(BF16), wgrp_ref[g]))
    y_pool = jnp.concatenate(y_groups, axis=1) * pscale_ref[...]
    pool_up = _dot(y_pool.astype(BF16), wpu_ref[...])
    g_pool = jax.nn.sigmoid(_dot(h, wgp_ref[...]))
    pb_ref[0] = (g_pool * pool_up).astype(BF16)
    ga_ref[0] = jax.nn.sigmoid(_dot(h, wga_ref[...])).astype(BF16)
    qkv = _dot(h, wqkv_ref[...])
    q_ref[0] = (qkv[:, :att_w] * (HEAD_DIM ** -0.5 * LOG2_E)).astype(BF16)
    k_ref[0] = qkv[:, att_w:2 * att_w].astype(BF16)
    v_ref[0] = qkv[:, 2 * att_w:].astype(BF16)


def _mix_in(x, g_mix, w_u, w_qkv, w_gp, w_ga, w_grp, pool_scale, w_pool_up, ts):
    b, s, d = x.shape
    pool_w = w_u.shape[1]
    att_w = w_qkv.shape[1] // 3
    const = lambda *shape: pl.BlockSpec(shape, lambda i, j: (0,) * len(shape))
    tile = lambda w: pl.BlockSpec((1, ts, w), lambda i, j: (i, j, 0))
    return pl.pallas_call(
        _mix_in_kernel,
        grid=(b, s // ts),
        in_specs=[tile(d), const(1, d), const(d, pool_w), const(d, 3 * att_w), const(d, d), const(d, d),
                  const(len(POOL_WINDOWS), POOL_CH, POOL_CH), const(1, pool_w), const(pool_w, d)],
        out_specs=[tile(att_w), tile(att_w), tile(att_w), tile(d), tile(d)],
        out_shape=[jax.ShapeDtypeStruct((b, s, att_w), BF16)] * 3 + [jax.ShapeDtypeStruct((b, s, d), BF16)] * 2,
        scratch_shapes=[pltpu.VMEM((POOL_HIST, pool_w), F32)],
        compiler_params=_params(2),
        name="mix_in",
    )(x, g_mix, w_u, w_qkv, w_gp, w_ga, w_grp, pool_scale, w_pool_up)


def _attn_kernel(q_ref, k_ref, v_ref, tri_ref, o_ref, *, blk):
    i = pl.program_id(2)
    q2 = q_ref[0]
    lane = lax.broadcasted_iota(jnp.int32, (blk, LANES), 1)
    row = lax.broadcasted_iota(jnp.int32, (blk, blk), 0)
    col = lax.broadcasted_iota(jnp.int32, (blk, blk), 1)
    causal = col < row
    heads = (lane < HEAD_DIM, lane >= HEAD_DIM)
    qs = [jnp.where(hm, q2, jnp.zeros_like(q2)) for hm in heads]

    def keys_values(j):
        start = pl.multiple_of(j * blk, blk)
        return k_ref[0, pl.ds(start, blk), :], v_ref[0, pl.ds(start, blk), :]

    def scores(qh, kj, diagonal):
        z = lax.dot_general(qh, kj, (((1,), (1,)), ((), ())), preferred_element_type=F32)
        softplus = jnp.maximum(z, 0.0) + jnp.log2(1.0 + jnp.exp2(-jnp.abs(z)))
        if diagonal:
            softplus = jnp.where(causal, softplus, 0.0)
        return z, _dot(softplus.astype(BF16), tri_ref[...])

    def accumulate(z, incl, acc, run, vj, diagonal):
        w = jnp.exp2(z - incl - jnp.concatenate([run] * (blk // LANES), axis=1))
        if diagonal:
            w = jnp.where(causal, w, 0.0)
        return acc + _dot(w.astype(BF16), vj), run + jnp.broadcast_to(incl[:, 0:1], (blk, LANES))

    has_prev = i > 0
    kd, vd = keys_values(i)
    kp, vp = keys_values(jnp.maximum(i - 1, 0))
    front_d = [scores(qh, kd, True) for qh in qs]
    front_p = [scores(qh, kp, False) for qh in qs]
    zero = jnp.zeros((blk, LANES), F32)
    accs, runs = [], []
    for (zd, incl_d), (zp, incl_p) in zip(front_d, front_p):
        acc_d, run_d = accumulate(zd, incl_d, zero, zero, vd, True)
        acc_p, run_p = accumulate(zp, incl_p, acc_d, run_d, vp, False)
        accs.append(jnp.where(has_prev, acc_p, acc_d))
        runs.append(jnp.where(has_prev, run_p, run_d))

    def live(c):
        n, _, _, min_run = c
        return jnp.logical_and(n < i, min_run < UNDERFLOW_BITS)

    def step(c):
        n, accs, runs, _ = c
        kj, vj = keys_values(i - 1 - n)
        front = [scores(qh, kj, False) for qh in qs]
        back = [accumulate(z, incl, acc, run, vj, False) for (z, incl), acc, run in zip(front, accs, runs)]
        accs, runs = [a for a, _ in back], [r for _, r in back]
        return n + 1, accs, runs, jnp.minimum(jnp.min(runs[0]), jnp.min(runs[1]))

    min_run = jnp.minimum(jnp.min(runs[0]), jnp.min(runs[1]))
    _, accs, _, _ = lax.while_loop(live, step, (jnp.int32(1), accs, runs, min_run))
    o_ref[0] = jnp.where(heads[0], accs[0], accs[1]).astype(o_ref.dtype)


def _attention(q, k, v, blk):
    b, s, att_w = q.shape
    tri = (jnp.arange(blk)[:, None] >= jnp.arange(blk)[None, :]).astype(BF16)
    qo_spec = pl.BlockSpec((1, blk, LANES), lambda bi, hp, i: (bi, i, hp))
    kv_spec = pl.BlockSpec((1, s, LANES), lambda bi, hp, i: (bi, 0, hp))
    return pl.pallas_call(
        functools.partial(_attn_kernel, blk=blk),
        grid=(b, att_w // LANES, s // blk),
        in_specs=[qo_spec, kv_spec, kv_spec, pl.BlockSpec((blk, blk), lambda bi, hp, i: (0, 0))],
        out_specs=qo_spec,
        out_shape=jax.ShapeDtypeStruct((b, s, att_w), BF16),
        compiler_params=_params(3),
        name="attn",
    )(q, k, v, tri)


def _mix_out_kernel(x_ref, ya_ref, pb_ref, ga_ref, wau_ref, wo_ref, gf_ref, wrh_ref, wrl_ref, br_ref, lt_ref, up_ref,
                    x1_ref, h2_ref, s_ref, w_ref, cnt_ref):
    t = x_ref.shape[0]

    att_up = _dot(ya_ref[...], wau_ref[...])
    merged = pb_ref[...].astype(F32) + ga_ref[...].astype(F32) * att_up
    x1 = x_ref[...] + _dot(merged.astype(BF16), wo_ref[...])
    x1_ref[...] = x1
    h2 = _rms(x1, gf_ref[...])
    h2_ref[...] = h2.astype(h2_ref.dtype)

    hi, lo = _split_bf16(h2)
    logits = _dot(hi, wrh_ref[...]) + _dot(lo, wrh_ref[...]) + _dot(hi, wrl_ref[...]) + br_ref[...]
    lane_e = lax.broadcasted_iota(jnp.int32, (t, N_EXPERTS), 1)
    vals, idxs = [], []
    for _ in range(TOP_K):
        m = jnp.max(logits, axis=-1, keepdims=True)
        idx = jnp.min(jnp.where(logits == m, lane_e, N_EXPERTS), axis=-1, keepdims=True)
        vals.append(m)
        idxs.append(idx)
        logits = jnp.where(lane_e == idx, -jnp.inf, logits)
    exps = [jnp.exp(m - vals[0]) for m in vals]
    denom = exps[0] + exps[1] + exps[2] + exps[3]

    onehot = jnp.zeros((t, N_EXPERTS), F32)
    for idx in idxs:
        onehot = onehot + jnp.where(lane_e == idx, 1.0, 0.0)
    pos = _dot(lt_ref[...], onehot.astype(BF16))
    sort_tile = t // cnt_ref.shape[0]
    seg_starts = []
    for j in range(cnt_ref.shape[0]):
        cnt = jnp.sum(onehot[j * sort_tile:(j + 1) * sort_tile], axis=0, keepdims=True)
        cnt_ref[j] = cnt
        seg_rows = jnp.ceil(cnt * (1.0 / SEG_ALIGN)) * SEG_ALIGN
        before = _dot(jnp.broadcast_to(seg_rows, (SEG_ALIGN, N_EXPERTS)).astype(BF16), up_ref[...])[0:1]
        seg_starts.append(jnp.broadcast_to(before, (sort_tile, N_EXPERTS)))
    pos = pos + jnp.concatenate(seg_starts, axis=0)

    lane = lax.broadcasted_iota(jnp.int32, (t, LANES), 1)
    s_out = jnp.zeros((t, LANES), jnp.int32)
    w_out = jnp.zeros((t, LANES), F32)
    for kk in range(TOP_K):
        slot = jnp.sum(jnp.where(lane_e == idxs[kk], pos, 0.0), axis=-1, keepdims=True).astype(jnp.int32)
        s_out = jnp.where(lane == kk, slot, s_out)
        w_out = jnp.where(lane == kk, exps[kk] / denom, w_out)
    s_ref[...] = s_out
    w_ref[...] = w_out


def _mix_out(x2d, y_att, pool_branch, g_att, w_attn_up, w_out, g_ffn, wr_hi, wr_lo, b_router, t, sort_tile):
    n, d = x2d.shape
    att_w = y_att.shape[1]
    r, c = jnp.arange(t)[:, None], jnp.arange(t)[None, :]
    lower = jnp.logical_and(r > c, r // sort_tile == c // sort_tile).astype(BF16)
    experts = jnp.arange(N_EXPERTS)
    before = (experts[:, None] < experts[None, :]).astype(BF16)
    const = lambda *shape: pl.BlockSpec(shape, lambda i: (0,) * len(shape))
    tile = lambda w: pl.BlockSpec((t, w), lambda i: (i, 0))
    return pl.pallas_call(
        _mix_out_kernel,
        grid=(n // t,),
        in_specs=[tile(d), tile(att_w), tile(d), tile(d), const(att_w, d), const(d, d), const(1, d),
                  const(d, N_EXPERTS), const(d, N_EXPERTS), const(1, N_EXPERTS), const(t, t),
                  const(N_EXPERTS, N_EXPERTS)],
        out_specs=[tile(d), tile(d), tile(LANES), tile(LANES),
                   pl.BlockSpec((t // sort_tile, 1, N_EXPERTS), lambda i: (i, 0, 0))],
        out_shape=[jax.ShapeDtypeStruct((n, d), F32), jax.ShapeDtypeStruct((n, d), BF16),
                   jax.ShapeDtypeStruct((n, LANES), jnp.int32), jax.ShapeDtypeStruct((n, LANES), F32),
                   jax.ShapeDtypeStruct((n // sort_tile, 1, N_EXPERTS), F32)],
        compiler_params=_params(1),
        name="mix_out",
    )(x2d, y_att, pool_branch, g_att, w_attn_up, w_out, g_ffn, wr_hi, wr_lo, b_router, lower, before)


def _for_segment_chunks(n8_ref, so_ref, go_ref, tile, act):
    def segment(e, c):
        n8 = n8_ref[tile * N_EXPERTS + e]
        in_tile = so_ref[tile * N_EXPERTS + e]
        in_buf = go_ref[tile * N_EXPERTS + e]
        n_big = lax.shift_right_logical(n8, BIG_CHUNK.bit_length() - 1)
        done = n_big * BIG_CHUNK

        def big(j, c2):
            act(pl.multiple_of(in_tile + j * BIG_CHUNK, SEG_ALIGN), pl.multiple_of(in_buf + j * BIG_CHUNK, SEG_ALIGN),
                BIG_CHUNK)
            return c2

        def small(j, c2):
            act(pl.multiple_of(in_tile + done + j * SEG_ALIGN, SEG_ALIGN),
                pl.multiple_of(in_buf + done + j * SEG_ALIGN, SEG_ALIGN), SEG_ALIGN)
            return c2

        lax.fori_loop(0, n_big, big, 0)
        lax.fori_loop(0, lax.shift_right_logical(n8 - done, SEG_ALIGN.bit_length() - 1), small, 0)
        return c

    lax.fori_loop(0, N_EXPERTS, segment, 0)


def _wait_rows(total, wait_chunk):
    n_big = lax.shift_right_logical(total, WAIT_CHUNK.bit_length() - 1)

    def big(j, c):
        wait_chunk(WAIT_CHUNK)
        return c

    lax.fori_loop(0, n_big, big, 0)
    rest = total - n_big * WAIT_CHUNK
    size = WAIT_CHUNK // 2
    while size >= SEG_ALIGN:
        @pl.when(jnp.bitwise_and(rest, size) != 0)
        def _(size=size):
            wait_chunk(size)
        size //= 2


def _dispatch_kernel(pe_ref, cnt_ref, n8_ref, so_ref, go_ref, tot_ref, slot_ref, h_ref, buf_ref, zero_ref, sorted_ref,
                     sems, zsem):
    t = h_ref.shape[0]
    n_slots = sorted_ref.shape[1]
    n_blocks = buf_ref.shape[0] // EXPERT_BLOCK
    i = pl.program_id(0)
    cur = lax.rem(i, 2)

    def copy(into, in_tile, in_buf, rows):
        return pltpu.make_async_copy(sorted_ref.at[into, pl.ds(in_tile, rows)], buf_ref.at[pl.ds(in_buf, rows)],
                                     sems.at[into])

    def wait_tile(tile, into):
        _wait_rows(tot_ref[tile], lambda rows: copy(into, 0, 0, rows).wait())

    @pl.when(i == 0)
    def _():
        zero_ref[...] = jnp.zeros_like(zero_ref)

        def zero_copy(start):
            start = pl.multiple_of(start, EXPERT_BLOCK)
            return pltpu.make_async_copy(zero_ref, buf_ref.at[pl.ds(start, EXPERT_BLOCK)], zsem)

        def start_last(e, c):
            @pl.when(cnt_ref[e] > 0)
            def _():
                zero_copy(pe_ref[e] - EXPERT_BLOCK).start()
            return c

        def wait_last(e, c):
            @pl.when(cnt_ref[e] > 0)
            def _():
                zero_copy(pe_ref[e] - EXPERT_BLOCK).wait()
            return c

        def start_tail(b, c):
            zero_copy(b * EXPERT_BLOCK).start()
            return c

        def wait_tail(b, c):
            zero_copy(b * EXPERT_BLOCK).wait()
            return c

        n_used = pe_ref[N_EXPERTS - 1] // EXPERT_BLOCK
        lax.fori_loop(0, N_EXPERTS, start_last, 0)
        lax.fori_loop(n_used, n_blocks, start_tail, 0)
        lax.fori_loop(0, N_EXPERTS, wait_last, 0)
        lax.fori_loop(n_used, n_blocks, wait_tail, 0)

    @pl.when(i >= 2)
    def _():
        wait_tile(i - 2, cur)

    slots = slot_ref[...]
    slot_id = lax.broadcasted_iota(jnp.int32, (t, n_slots), 1)
    hit = slot_id == slots[:, 0:1]
    for kk in range(1, TOP_K):
        hit = jnp.logical_or(hit, slot_id == slots[:, kk:kk + 1])
    select = jnp.where(hit, 1.0, 0.0).astype(BF16)
    sorted_ref[cur] = lax.dot_general(select, h_ref[...], (((0,), (0,)), ((), ())), preferred_element_type=F32)
    _for_segment_chunks(n8_ref, so_ref, go_ref, i,
                        lambda in_tile, in_buf, rows: copy(cur, in_tile, in_buf, rows).start())

    @pl.when(i == pl.num_programs(0) - 1)
    def _():
        @pl.when(i >= 1)
        def _():
            wait_tile(i - 1, 1 - cur)
        wait_tile(i, cur)


def _dispatch(h2, slots, pad_ends, counts, seg_rows, seg_src, seg_dst, tile_rows, n_pad, t):
    n, d = h2.shape
    n_slots = TOP_K * t + N_EXPERTS * SEG_ALIGN
    return pl.pallas_call(
        _dispatch_kernel,
        grid_spec=pltpu.PrefetchScalarGridSpec(
            num_scalar_prefetch=6,
            grid=(n // t,),
            in_specs=[pl.BlockSpec((t, LANES), lambda i, *_: (i, 0)), pl.BlockSpec((t, d), lambda i, *_: (i, 0))],
            out_specs=pl.BlockSpec(memory_space=pl.ANY),
            scratch_shapes=[pltpu.VMEM((EXPERT_BLOCK, d), F32), pltpu.VMEM((2, n_slots, d), F32),
                            pltpu.SemaphoreType.DMA((2,)), pltpu.SemaphoreType.DMA],
        ),
        out_shape=jax.ShapeDtypeStruct((n_pad, d), F32),
        compiler_params=_params(1),
        name="dispatch",
    )(pad_ends, counts, seg_rows, seg_src, seg_dst, tile_rows, slots, h2)


def _expert_kernel(be_ref, nu_ref, x_ref, wgu_ref, bgu_ref, wd_ref, bd_ref, y_ref, wgu_bf, wd_bf):
    d_exp = wd_ref.shape[1]
    i = pl.program_id(0)

    @pl.when(jnp.logical_or(i == 0, be_ref[i] != be_ref[jnp.maximum(i - 1, 0)]))
    def _():
        wgu_bf[...] = wgu_ref[0].astype(BF16)
        wd_bf[...] = wd_ref[0].astype(BF16)

    @pl.when(i < nu_ref[0])
    def _():
        gu = _dot(x_ref[...].astype(BF16), wgu_bf[...]) + bgu_ref[0]
        gate = jnp.minimum(gu[:, :d_exp], SWIGLU_LIMIT)
        up = jnp.clip(gu[:, d_exp:], -SWIGLU_LIMIT, SWIGLU_LIMIT)
        glu = gate * jax.nn.sigmoid(SWIGLU_ALPHA * gate)
        y_ref[...] = _dot(((up + 1.0) * glu).astype(BF16), wd_bf[...]) + bd_ref[0]

    @pl.when(i >= nu_ref[0])
    def _():
        y_ref[...] = jnp.zeros_like(y_ref)


def _experts(buf, block_e, n_used, w_gu, b_gu, w_down, b_down):
    n_pad, d = buf.shape
    d_exp = w_down.shape[1]
    used = lambda i, be, nu: jnp.minimum(i, nu[0] - 1)
    rows = pl.BlockSpec((EXPERT_BLOCK, d), lambda i, be, nu: (used(i, be, nu), 0))
    out_rows = pl.BlockSpec((EXPERT_BLOCK, d), lambda i, be, nu: (i, 0))
    per_e = lambda *shape: pl.BlockSpec((1,) + shape, lambda i, be, nu: (be[used(i, be, nu)], 0, 0))
    return pl.pallas_call(
        _expert_kernel,
        grid_spec=pltpu.PrefetchScalarGridSpec(
            num_scalar_prefetch=2,
            grid=(n_pad // EXPERT_BLOCK,),
            in_specs=[rows, per_e(d, 2 * d_exp), per_e(1, 2 * d_exp), per_e(d_exp, d), per_e(1, d)],
            out_specs=out_rows,
            scratch_shapes=[pltpu.VMEM((d, 2 * d_exp), BF16), pltpu.VMEM((d_exp, d), BF16)],
        ),
        out_shape=jax.ShapeDtypeStruct((n_pad, d), F32),
        compiler_params=_params(1),
        name="experts",
    )(block_e, n_used, buf, w_gu, b_gu, w_down, b_down)


def _combine_kernel(n8_ref, so_ref, go_ref, tot_ref, slot_ref, tw_ref, x1_ref, p_ref, gp_ref, wg_ref, wp_ref, gfin_ref,
                    y_ref, o_ref, rows_ref, sems, *, final_norm):
    t = x1_ref.shape[0]
    n_slots = rows_ref.shape[1]
    i = pl.program_id(0)
    cur = lax.rem(i, 2)

    def copy(into, in_tile, in_buf, rows):
        return pltpu.make_async_copy(y_ref.at[pl.ds(in_buf, rows)], rows_ref.at[into, pl.ds(in_tile, rows)],
                                     sems.at[into])

    def start_tile(tile, into):
        _for_segment_chunks(n8_ref, so_ref, go_ref, tile,
                            lambda in_tile, in_buf, rows: copy(into, in_tile, in_buf, rows).start())

    @pl.when(i == 0)
    def _():
        rows_ref[...] = jnp.zeros_like(rows_ref)
        start_tile(0, 0)

    @pl.when(i + 1 < pl.num_programs(0))
    def _():
        start_tile(i + 1, 1 - cur)

    _wait_rows(tot_ref[i], lambda rows: copy(cur, 0, 0, rows).wait())

    slots = slot_ref[...]
    tw = tw_ref[...]
    slot_id = lax.broadcasted_iota(jnp.int32, (t, n_slots), 1)
    weights = jnp.zeros((t, n_slots), F32)
    for kk in range(TOP_K):
        weights = jnp.where(slot_id == slots[:, kk:kk + 1], tw[:, kk:kk + 1], weights)
    w_hi, w_lo = _split_bf16(weights)
    y = rows_ref[cur].astype(BF16)
    x2 = x1_ref[...] + (_dot(w_hi, y) + _dot(w_lo, y))
    hp = _rms(x2, gp_ref[...]).astype(BF16)
    gate = jax.nn.sigmoid(_dot(hp, wg_ref[...]))
    x3 = x2 + gate * _dot(p_ref[...].astype(BF16), wp_ref[...])
    o_ref[...] = _rms(x3, gfin_ref[...]) if final_norm else x3


def _combine(slots, top_w, x1, p2d, y_buf, seg_rows, seg_src, seg_dst, tile_rows, g_ple, w_ple_gate, w_ple_proj,
             g_final, final_norm, t):
    n, d = x1.shape
    ple = p2d.shape[1]
    n_slots = TOP_K * t + N_EXPERTS * SEG_ALIGN
    const = lambda *shape: pl.BlockSpec(shape, lambda i, *_: (0,) * len(shape))
    tile = lambda w: pl.BlockSpec((t, w), lambda i, *_: (i, 0))
    return pl.pallas_call(
        functools.partial(_combine_kernel, final_norm=final_norm),
        grid_spec=pltpu.PrefetchScalarGridSpec(
            num_scalar_prefetch=4,
            grid=(n // t,),
            in_specs=[tile(LANES), tile(LANES), tile(d), tile(ple), const(1, d), const(d, d), const(ple, d),
                      const(1, d), pl.BlockSpec(memory_space=pl.ANY)],
            out_specs=tile(d),
            scratch_shapes=[pltpu.VMEM((2, n_slots, d), F32), pltpu.SemaphoreType.DMA((2,))],
        ),
        out_shape=jax.ShapeDtypeStruct((n, d), F32),
        compiler_params=_params(1),
        name="combine",
    )(seg_rows, seg_src, seg_dst, tile_rows, slots, top_w, x1, p2d, g_ple, w_ple_gate, w_ple_proj, g_final, y_buf)


def _layer(x, p, g_mix, w_in, w_pool_grp, pool_scale, w_pool_up, w_attn_up, w_out, g_ffn, w_router, b_router,
           w_gu, b_gu, w_down, b_down, g_ple, w_ple_gate, w_ple_proj, g_final, final_norm):
    b, s, d = x.shape
    n = b * s
    pool_w = len(POOL_WINDOWS) * POOL_CH
    att_w = N_HEADS * HEAD_DIM
    seq_tile = min(ROW_TILE, s)
    tok_tile = min(ROW_TILE, n)
    sort_tile = min(SORT_TILE, n)
    attn_blk = min(ATTN_BLOCK, s)

    row = lambda v: v.reshape(1, -1)
    w_in = w_in.astype(BF16)
    w_u, w_qkv = w_in[:, :pool_w], w_in[:, pool_w:pool_w + 3 * att_w]
    w_gp, w_ga = w_in[:, pool_w + 3 * att_w:pool_w + 3 * att_w + d], w_in[:, pool_w + 3 * att_w + d:]
    q, k, v, pool_branch, g_att = _mix_in(x, row(g_mix), w_u, w_qkv, w_gp, w_ga, w_pool_grp.astype(BF16),
                                          row(pool_scale), w_pool_up.astype(BF16), seq_tile)
    y_att = _attention(q, k, v, attn_blk)

    wr_hi = w_router.astype(BF16)
    wr_lo = (w_router - wr_hi.astype(F32)).astype(BF16)
    x1, h2, slots, w_top, tile_counts = _mix_out(
        x.reshape(n, d), y_att.reshape(n, att_w), pool_branch.reshape(n, d), g_att.reshape(n, d),
        w_attn_up.astype(BF16), w_out.astype(BF16), row(g_ffn), wr_hi, wr_lo, row(b_router), tok_tile, sort_tile)

    n_tiles = n // sort_tile
    seg_rows = -(-tile_counts[:, 0, :].astype(jnp.int32) // SEG_ALIGN) * SEG_ALIGN
    seg_src = jnp.cumsum(seg_rows, axis=1) - seg_rows
    group_rows = jnp.sum(seg_rows, axis=0)
    padded = (group_rows + EXPERT_BLOCK - 1) // EXPERT_BLOCK * EXPERT_BLOCK
    pad_ends = jnp.cumsum(padded)
    seg_dst = (pad_ends - padded)[None, :] + jnp.cumsum(seg_rows, axis=0) - seg_rows
    n_blocks = -(-(n * TOP_K + n_tiles * N_EXPERTS * (SEG_ALIGN - 1)) // EXPERT_BLOCK) + N_EXPERTS
    block_starts = jnp.arange(n_blocks, dtype=jnp.int32) * EXPERT_BLOCK
    block_e = jnp.minimum(jnp.sum((pad_ends[None, :] <= block_starts[:, None]).astype(jnp.int32), axis=1),
                          N_EXPERTS - 1)
    n_used = (pad_ends[-1:] // EXPERT_BLOCK).astype(jnp.int32)
    seg = (seg_rows.reshape(-1), seg_src.reshape(-1), seg_dst.reshape(-1), jnp.sum(seg_rows, axis=1))

    buf = _dispatch(h2, slots, pad_ends, group_rows, *seg, n_blocks * EXPERT_BLOCK, sort_tile)
    y_buf = _experts(buf, block_e, n_used, w_gu, b_gu[:, None, :], w_down, b_down[:, None, :])
    out = _combine(slots, w_top, x1, p.reshape(n, -1), y_buf, *seg, row(g_ple), w_ple_gate.astype(BF16),
                   w_ple_proj.astype(BF16), row(g_final), final_norm, sort_tile)
    return out.reshape(b, s, d)


def kernel(x, p, g_mix, w_in, w_pool_grp, pool_scale, w_pool_up, w_attn_up, w_out, g_ffn, w_router, b_router,
           w_gu, b_gu, w_down, b_down, g_ple, w_ple_gate, w_ple_proj, g_final):
    depth = p.shape[0]
    for i in range(depth):
        x = _layer(x, p[i], g_mix[i], w_in[i], w_pool_grp[i], pool_scale[i], w_pool_up[i], w_attn_up[i], w_out[i],
                   g_ffn[i], w_router[i], b_router[i], w_gu[i], b_gu[i], w_down[i], b_down[i], g_ple[i],
                   w_ple_gate[i], w_ple_proj[i], g_final, i == depth - 1)
    return x
```

```python
import functools

import jax
import jax.numpy as jnp
from jax import lax
from jax.experimental import pallas as pl
from jax.experimental.pallas import tpu as pltpu

F32 = jnp.float32
BF16 = jnp.bfloat16

N_HEADS = 8
HEAD_DIM = 64
POOL_WINDOWS = (2, 4, 8, 16)
POOL_CH = 128
POOL_HIST = 16
N_EXPERTS = 32
TOP_K = 4
EXPERT_BLOCK = 512
SEG_ALIGN = 8
BIG_CHUNK = 32
WAIT_CHUNK = 256
SMALL_CAP = N_EXPERTS * (BIG_CHUNK // SEG_ALIGN - 1)
SRC_BITS = 12
SWIGLU_LIMIT = 7.0
SWIGLU_ALPHA = 1.702
EPS = 1e-6
LOG2_E = 1.4426950408889634
UNDERFLOW_BITS = 160.0
LANES = 128
ROW_TILE = 512
SORT_TILE = 256
ATTN_BLOCK = 256
VMEM_LIMIT = 56 * 1024 * 1024


def _params(n_axes, flags=None):
    return pltpu.CompilerParams(dimension_semantics=("arbitrary",) * n_axes, vmem_limit_bytes=VMEM_LIMIT,
                                flags=flags)


def _rms(xf, g):
    return xf * lax.rsqrt(jnp.mean(xf * xf, axis=-1, keepdims=True) + EPS) * g


def _dot(a, b):
    return jnp.dot(a, b, preferred_element_type=F32)


def _split_bf16(a):
    hi = a.astype(BF16)
    lo = (a - hi.astype(F32)).astype(BF16)
    return hi, lo


def _mix_in_kernel(x_ref, g_ref, wu_ref, wqkv_ref, wgp_ref, wga_ref, wgrp_ref, pscale_ref, wpu_ref,
                   q_ref, k_ref, v_ref, pb_ref, ga_ref, hist_ref):
    s = pl.program_id(1)
    ts = x_ref.shape[1]
    att_w = q_ref.shape[2]

    @pl.when(s == 0)
    def _():
        hist_ref[...] = jnp.zeros_like(hist_ref)

    h = _rms(x_ref[0], g_ref[...]).astype(BF16)
    u = _dot(h, wu_ref[...])
    ext = jnp.concatenate([hist_ref[...], u], axis=0)
    hist_ref[...] = u[ts - POOL_HIST:, :]
    t_pos = s * ts + lax.broadcasted_iota(jnp.int32, (ts, 1), 0)
    y_groups = []
    for g, w in enumerate(POOL_WINDOWS):
        a = ext[:, g * POOL_CH:(g + 1) * POOL_CH]
        sh = 1
        while sh < w:
            a = a + pltpu.roll(a, sh, axis=0)
            sh *= 2
        count = jnp.minimum(t_pos + 1, w).astype(F32)
        pooled = a[POOL_HIST:, :] / count - u[:, g * POOL_CH:(g + 1) * POOL_CH]
        y_groups.append(_dot(pooled.astype(BF16), wgrp_ref[g]))
    y_pool = jnp.concatenate(y_groups, axis=1) * pscale_ref[...]
    pool_up = _dot(y_pool.astype(BF16), wpu_ref[...])
    g_pool = jax.nn.sigmoid(_dot(h, wgp_ref[...]))
    pb_ref[0] = (g_pool * pool_up).astype(BF16)
    ga_ref[0] = jax.nn.sigmoid(_dot(h, wga_ref[...])).astype(BF16)
    qkv = _dot(h, wqkv_ref[...])
    q_ref[0] = (qkv[:, :att_w] * (HEAD_DIM ** -0.5 * LOG2_E)).astype(BF16)
    k_ref[0] = qkv[:, att_w:2 * att_w].astype(BF16)
    v_ref[0] = qkv[:, 2 * att_w:].astype(BF16)


def _mix_in(x, g_mix, w_u, w_qkv, w_gp, w_ga, w_grp, pool_scale, w_pool_up, ts):
    b, s, d = x.shape
    pool_w = w_u.shape[1]
    att_w = w_qkv.shape[1] // 3
    const = lambda *shape: pl.BlockSpec(shape, lambda i, j: (0,) * len(shape))
    tile = lambda w: pl.BlockSpec((1, ts, w), lambda i, j: (i, j, 0))
    return pl.pallas_call(
        _mix_in_kernel,
        grid=(b, s // ts),
        in_specs=[tile(d), const(1, d), const(d, pool_w), const(d, 3 * att_w), const(d, d), const(d, d),
                  const(len(POOL_WINDOWS), POOL_CH, POOL_CH), const(1, pool_w), const(pool_w, d)],
        out_specs=[tile(att_w), tile(att_w), tile(att_w), tile(d), tile(d)],
        out_shape=[jax.ShapeDtypeStruct((b, s, att_w), BF16)] * 3 + [jax.ShapeDtypeStruct((b, s, d), BF16)] * 2,
        scratch_shapes=[pltpu.VMEM((POOL_HIST, pool_w), F32)],
        compiler_params=_params(2),
        name="mix_in",
    )(x, g_mix, w_u, w_qkv, w_gp, w_ga, w_grp, pool_scale, w_pool_up)


def _attn_kernel(q_ref, k_ref, v_ref, tri_ref, o_ref, *, blk):
    i = pl.program_id(2)
    q2 = q_ref[0]
    lane = lax.broadcasted_iota(jnp.int32, (blk, LANES), 1)
    row = lax.broadcasted_iota(jnp.int32, (blk, blk), 0)
    col = lax.broadcasted_iota(jnp.int32, (blk, blk), 1)
    causal = col < row
    heads = (lane < HEAD_DIM, lane >= HEAD_DIM)
    qs = [jnp.where(hm, q2, jnp.zeros_like(q2)) for hm in heads]

    def keys_values(j):
        start = pl.multiple_of(j * blk, blk)
        return k_ref[0, pl.ds(start, blk), :], v_ref[0, pl.ds(start, blk), :]

    def scores(qh, kj, diagonal):
        z = lax.dot_general(qh, kj, (((1,), (1,)), ((), ())), preferred_element_type=F32)
        softplus = jnp.maximum(z, 0.0) + jnp.log2(1.0 + jnp.exp2(-jnp.abs(z)))
        if diagonal:
            softplus = jnp.where(causal, softplus, 0.0)
        return z, _dot(softplus.astype(BF16), tri_ref[...])

    def accumulate(z, incl, acc, run, vj, diagonal):
        w = jnp.exp2(z - incl - jnp.concatenate([run] * (blk // LANES), axis=1))
        if diagonal:
            w = jnp.where(causal, w, 0.0)
        return acc + _dot(w.astype(BF16), vj), run + jnp.broadcast_to(incl[:, 0:1], (blk, LANES))

    has_prev = i > 0
    kd, vd = keys_values(i)
    kp, vp = keys_values(jnp.maximum(i - 1, 0))
    front_d = [scores(qh, kd, True) for qh in qs]
    front_p = [scores(qh, kp, False) for qh in qs]
    zero = jnp.zeros((blk, LANES), F32)
    accs, runs = [], []
    for (zd, incl_d), (zp, incl_p) in zip(front_d, front_p):
        acc_d, run_d = accumulate(zd, incl_d, zero, zero, vd, True)
        acc_p, run_p = accumulate(zp, incl_p, acc_d, run_d, vp, False)
        accs.append(jnp.where(has_prev, acc_p, acc_d))
        runs.append(jnp.where(has_prev, run_p, run_d))

    def live(c):
        n, _, _, min_run = c
        return jnp.logical_and(n < i, min_run < UNDERFLOW_BITS)

    def step(c):
        n, accs, runs, _ = c
        kj, vj = keys_values(i - 1 - n)
        front = [scores(qh, kj, False) for qh in qs]
        back = [accumulate(z, incl, acc, run, vj, False) for (z, incl), acc, run in zip(front, accs, runs)]
        accs, runs = [a for a, _ in back], [r for _, r in back]
        return n + 1, accs, runs, jnp.minimum(jnp.min(runs[0]), jnp.min(runs[1]))

    min_run = jnp.minimum(jnp.min(runs[0]), jnp.min(runs[1]))
    _, accs, _, _ = lax.while_loop(live, step, (jnp.int32(1), accs, runs, min_run))
    o_ref[0] = jnp.where(heads[0], accs[0], accs[1]).astype(o_ref.dtype)


def _attention(q, k, v, blk):
    b, s, att_w = q.shape
    tri = (jnp.arange(blk)[:, None] >= jnp.arange(blk)[None, :]).astype(BF16)
    qo_spec = pl.BlockSpec((1, blk, LANES), lambda bi, hp, i: (bi, i, hp))
    kv_spec = pl.BlockSpec((1, s, LANES), lambda bi, hp, i: (bi, 0, hp))
    return pl.pallas_call(
        functools.partial(_attn_kernel, blk=blk),
        grid=(b, att_w // LANES, s // blk),
        in_specs=[qo_spec, kv_spec, kv_spec, pl.BlockSpec((blk, blk), lambda bi, hp, i: (0, 0))],
        out_specs=qo_spec,
        out_shape=jax.ShapeDtypeStruct((b, s, att_w), BF16),
        compiler_params=_params(3),
        name="attn",
    )(q, k, v, tri)


def _mix_out_kernel(x_ref, ya_ref, pb_ref, ga_ref, wau_ref, wo_ref, gf_ref, wrh_ref, wrl_ref, br_ref, lt_ref, up_ref,
                    x1_ref, h2_ref, s_ref, w_ref, cnt_ref):
    t = x_ref.shape[0]

    att_up = _dot(ya_ref[...], wau_ref[...])
    merged = pb_ref[...].astype(F32) + ga_ref[...].astype(F32) * att_up
    x1 = x_ref[...] + _dot(merged.astype(BF16), wo_ref[...])
    x1_ref[...] = x1
    h2 = _rms(x1, gf_ref[...])
    h2_ref[...] = h2.astype(h2_ref.dtype)

    hi, lo = _split_bf16(h2)
    logits = _dot(hi, wrh_ref[...]) + _dot(lo, wrh_ref[...]) + _dot(hi, wrl_ref[...]) + br_ref[...]
    lane_e = lax.broadcasted_iota(jnp.int32, (t, N_EXPERTS), 1)
    vals, idxs = [], []
    for _ in range(TOP_K):
        m = jnp.max(logits, axis=-1, keepdims=True)
        idx = jnp.min(jnp.where(logits == m, lane_e, N_EXPERTS), axis=-1, keepdims=True)
        vals.append(m)
        idxs.append(idx)
        logits = jnp.where(lane_e == idx, -jnp.inf, logits)
    exps = [jnp.exp(m - vals[0]) for m in vals]
    denom = exps[0] + exps[1] + exps[2] + exps[3]

    onehot = jnp.zeros((t, N_EXPERTS), F32)
    for idx in idxs:
        onehot = onehot + jnp.where(lane_e == idx, 1.0, 0.0)
    pos = _dot(lt_ref[...], onehot.astype(BF16))
    sort_tile = t // cnt_ref.shape[0]
    seg_starts = []
    for j in range(cnt_ref.shape[0]):
        cnt = jnp.sum(onehot[j * sort_tile:(j + 1) * sort_tile], axis=0, keepdims=True)
        cnt_ref[j] = cnt
        seg_rows = jnp.ceil(cnt * (1.0 / SEG_ALIGN)) * SEG_ALIGN
        before = _dot(jnp.broadcast_to(seg_rows, (SEG_ALIGN, N_EXPERTS)).astype(BF16), up_ref[...])[0:1]
        seg_starts.append(jnp.broadcast_to(before, (sort_tile, N_EXPERTS)))
    pos = pos + jnp.concatenate(seg_starts, axis=0)

    lane = lax.broadcasted_iota(jnp.int32, (t, LANES), 1)
    s_out = jnp.zeros((t, LANES), jnp.int32)
    w_out = jnp.zeros((t, LANES), F32)
    for kk in range(TOP_K):
        slot = jnp.sum(jnp.where(lane_e == idxs[kk], pos, 0.0), axis=-1, keepdims=True).astype(jnp.int32)
        s_out = jnp.where(lane == kk, slot, s_out)
        w_out = jnp.where(lane == kk, exps[kk] / denom, w_out)
    s_ref[...] = s_out
    w_ref[...] = w_out


def _mix_out(x2d, y_att, pool_branch, g_att, w_attn_up, w_out, g_ffn, wr_hi, wr_lo, b_router, t, sort_tile):
    n, d = x2d.shape
    att_w = y_att.shape[1]
    r, c = jnp.arange(t)[:, None], jnp.arange(t)[None, :]
    lower = jnp.logical_and(r > c, r // sort_tile == c // sort_tile).astype(BF16)
    experts = jnp.arange(N_EXPERTS)
    before = (experts[:, None] < experts[None, :]).astype(BF16)
    const = lambda *shape: pl.BlockSpec(shape, lambda i: (0,) * len(shape))
    tile = lambda w: pl.BlockSpec((t, w), lambda i: (i, 0))
    return pl.pallas_call(
        _mix_out_kernel,
        grid=(n // t,),
        in_specs=[tile(d), tile(att_w), tile(d), tile(d), const(att_w, d), const(d, d), const(1, d),
                  const(d, N_EXPERTS), const(d, N_EXPERTS), const(1, N_EXPERTS), const(t, t),
                  const(N_EXPERTS, N_EXPERTS)],
        out_specs=[tile(d), tile(d), tile(LANES), tile(LANES),
                   pl.BlockSpec((t // sort_tile, 1, N_EXPERTS), lambda i: (i, 0, 0))],
        out_shape=[jax.ShapeDtypeStruct((n, d), F32), jax.ShapeDtypeStruct((n, d), BF16),
                   jax.ShapeDtypeStruct((n, LANES), jnp.int32), jax.ShapeDtypeStruct((n, LANES), F32),
                   jax.ShapeDtypeStruct((n // sort_tile, 1, N_EXPERTS), F32)],
        compiler_params=_params(1),
        name="mix_out",
    )(x2d, y_att, pool_branch, g_att, w_attn_up, w_out, g_ffn, wr_hi, wr_lo, b_router, lower, before)


def _chunk_table(seg_rows, seg_src, seg_dst, sort_tile):
    n_big = seg_rows // BIG_CHUNK
    n_small = (seg_rows - n_big * BIG_CHUNK) // SEG_ALIGN

    def copies(counts, first_row, rows, capacity):
        ends = jnp.cumsum(counts, axis=1)
        starts = ends - counts
        idx = jnp.arange(capacity, dtype=jnp.int32)[None, :, None]
        mine = jnp.logical_and(starts[:, None, :] <= idx, idx < ends[:, None, :])
        pick = lambda a: jnp.sum(jnp.where(mine, a[:, None, :], 0), axis=2)
        row = pick(first_row) + (idx[:, :, 0] - pick(starts)) * rows
        return (pick(seg_src) + row) // SEG_ALIGN + ((pick(seg_dst) + row) // SEG_ALIGN << SRC_BITS)

    big_cap = (TOP_K * sort_tile + N_EXPERTS * (SEG_ALIGN - 1)) // BIG_CHUNK
    table = jnp.concatenate([jnp.sum(n_big, axis=1, keepdims=True), jnp.sum(n_small, axis=1, keepdims=True),
                             copies(n_big, jnp.zeros_like(n_big), BIG_CHUNK, big_cap),
                             copies(n_small, n_big * BIG_CHUNK, SEG_ALIGN, SMALL_CAP)], axis=1)
    return table[:, None, :].astype(jnp.int32)


def _for_tile_chunks(tab_ref, act):
    big_cap = tab_ref.shape[2] - 2 - SMALL_CAP

    def run(first, count, rows):
        def body(c, carry):
            packed = tab_ref[0, 0, first + c]
            act(pl.multiple_of(jnp.bitwise_and(packed, (1 << SRC_BITS) - 1) * SEG_ALIGN, SEG_ALIGN),
                pl.multiple_of(lax.shift_right_logical(packed, SRC_BITS) * SEG_ALIGN, SEG_ALIGN), rows)
            return carry
        lax.fori_loop(0, count, body, 0)

    run(2, tab_ref[0, 0, 0], BIG_CHUNK)
    run(2 + big_cap, tab_ref[0, 0, 1], SEG_ALIGN)


def _wait_rows(total, wait_chunk):
    n_big = lax.shift_right_logical(total, WAIT_CHUNK.bit_length() - 1)

    def big(j, c):
        wait_chunk(WAIT_CHUNK)
        return c

    lax.fori_loop(0, n_big, big, 0)
    rest = total - n_big * WAIT_CHUNK
    size = WAIT_CHUNK // 2
    while size >= SEG_ALIGN:
        @pl.when(jnp.bitwise_and(rest, size) != 0)
        def _(size=size):
            wait_chunk(size)
        size //= 2


def _dispatch_kernel(pe_ref, cnt_ref, tot_ref, tab_ref, slot_ref, h_ref, buf_ref, zero_ref, sorted_ref, sems, zsem):
    t = h_ref.shape[0]
    n_slots = sorted_ref.shape[1]
    n_blocks = buf_ref.shape[0] // EXPERT_BLOCK
    i = pl.program_id(0)
    cur = lax.rem(i, 2)

    def copy(into, in_tile, in_buf, rows):
        return pltpu.make_async_copy(sorted_ref.at[into, pl.ds(in_tile, rows)], buf_ref.at[pl.ds(in_buf, rows)],
                                     sems.at[into])

    def wait_tile(tile, into):
        _wait_rows(tot_ref[tile], lambda rows: copy(into, 0, 0, rows).wait())

    @pl.when(i == 0)
    def _():
        zero_ref[...] = jnp.zeros_like(zero_ref)

        def zero_copy(start):
            start = pl.multiple_of(start, EXPERT_BLOCK)
            return pltpu.make_async_copy(zero_ref, buf_ref.at[pl.ds(start, EXPERT_BLOCK)], zsem)

        def start_last(e, c):
            @pl.when(cnt_ref[e] > 0)
            def _():
                zero_copy(pe_ref[e] - EXPERT_BLOCK).start()
            return c

        def wait_last(e, c):
            @pl.when(cnt_ref[e] > 0)
            def _():
                zero_copy(pe_ref[e] - EXPERT_BLOCK).wait()
            return c

        def start_tail(b, c):
            zero_copy(b * EXPERT_BLOCK).start()
            return c

        def wait_tail(b, c):
            zero_copy(b * EXPERT_BLOCK).wait()
            return c

        n_used = pe_ref[N_EXPERTS - 1] // EXPERT_BLOCK
        lax.fori_loop(0, N_EXPERTS, start_last, 0)
        lax.fori_loop(n_used, n_blocks, start_tail, 0)
        lax.fori_loop(0, N_EXPERTS, wait_last, 0)
        lax.fori_loop(n_used, n_blocks, wait_tail, 0)

    @pl.when(i >= 2)
    def _():
        wait_tile(i - 2, cur)

    slots = slot_ref[...]
    slot_id = lax.broadcasted_iota(jnp.int32, (t, n_slots), 1)
    hit = slot_id == slots[:, 0:1]
    for kk in range(1, TOP_K):
        hit = jnp.logical_or(hit, slot_id == slots[:, kk:kk + 1])
    select = jnp.where(hit, 1.0, 0.0).astype(BF16)
    sorted_ref[cur] = lax.dot_general(select, h_ref[...], (((0,), (0,)), ((), ())), preferred_element_type=F32)
    _for_tile_chunks(tab_ref, lambda in_tile, in_buf, rows: copy(cur, in_tile, in_buf, rows).start())

    @pl.when(i == pl.num_programs(0) - 1)
    def _():
        @pl.when(i >= 1)
        def _():
            wait_tile(i - 1, 1 - cur)
        wait_tile(i, cur)


def _dispatch(h2, slots, pad_ends, counts, tile_rows, chunks, n_pad, t):
    n, d = h2.shape
    n_slots = TOP_K * t + N_EXPERTS * SEG_ALIGN
    assert n_slots // SEG_ALIGN <= 1 << SRC_BITS and n_pad // SEG_ALIGN < 1 << (31 - SRC_BITS)
    return pl.pallas_call(
        _dispatch_kernel,
        grid_spec=pltpu.PrefetchScalarGridSpec(
            num_scalar_prefetch=3,
            grid=(n // t,),
            in_specs=[pl.BlockSpec((1, 1, chunks.shape[2]), lambda i, *_: (i, 0, 0), memory_space=pltpu.SMEM),
                      pl.BlockSpec((t, LANES), lambda i, *_: (i, 0)), pl.BlockSpec((t, d), lambda i, *_: (i, 0))],
            out_specs=pl.BlockSpec(memory_space=pl.ANY),
            scratch_shapes=[pltpu.VMEM((EXPERT_BLOCK, d), F32), pltpu.VMEM((2, n_slots, d), F32),
                            pltpu.SemaphoreType.DMA((2,)), pltpu.SemaphoreType.DMA],
        ),
        out_shape=jax.ShapeDtypeStruct((n_pad, d), F32),
        compiler_params=_params(1),
        name="dispatch",
    )(pad_ends, counts, tile_rows, chunks, slots, h2)


def _expert_kernel(be_ref, nu_ref, x_ref, wgu_ref, bgu_ref, wd_ref, bd_ref, y_ref, wgu_bf, wd_bf):
    d_exp = wd_ref.shape[1]
    i = pl.program_id(0)

    @pl.when(jnp.logical_or(i == 0, be_ref[i] != be_ref[jnp.maximum(i - 1, 0)]))
    def _():
        wgu_bf[...] = wgu_ref[0].astype(BF16)
        wd_bf[...] = wd_ref[0].astype(BF16)

    @pl.when(i < nu_ref[0])
    def _():
        gu = _dot(x_ref[...].astype(BF16), wgu_bf[...]) + bgu_ref[0]
        gate = jnp.minimum(gu[:, :d_exp], SWIGLU_LIMIT)
        up = jnp.clip(gu[:, d_exp:], -SWIGLU_LIMIT, SWIGLU_LIMIT)
        glu = gate * jax.nn.sigmoid(SWIGLU_ALPHA * gate)
        y_ref[...] = _dot(((up + 1.0) * glu).astype(BF16), wd_bf[...]) + bd_ref[0]

    @pl.when(i >= nu_ref[0])
    def _():
        y_ref[...] = jnp.zeros_like(y_ref)


def _experts(buf, block_e, n_used, w_gu, b_gu, w_down, b_down):
    n_pad, d = buf.shape
    d_exp = w_down.shape[1]
    used = lambda i, be, nu: jnp.minimum(i, nu[0] - 1)
    rows = pl.BlockSpec((EXPERT_BLOCK, d), lambda i, be, nu: (used(i, be, nu), 0))
    out_rows = pl.BlockSpec((EXPERT_BLOCK, d), lambda i, be, nu: (i, 0))
    per_e = lambda *shape: pl.BlockSpec((1,) + shape, lambda i, be, nu: (be[used(i, be, nu)], 0, 0))
    return pl.pallas_call(
        _expert_kernel,
        grid_spec=pltpu.PrefetchScalarGridSpec(
            num_scalar_prefetch=2,
            grid=(n_pad // EXPERT_BLOCK,),
            in_specs=[rows, per_e(d, 2 * d_exp), per_e(1, 2 * d_exp), per_e(d_exp, d), per_e(1, d)],
            out_specs=out_rows,
            scratch_shapes=[pltpu.VMEM((d, 2 * d_exp), BF16), pltpu.VMEM((d_exp, d), BF16)],
        ),
        out_shape=jax.ShapeDtypeStruct((n_pad, d), F32),
        compiler_params=_params(1),
        name="experts",
    )(block_e, n_used, buf, w_gu, b_gu, w_down, b_down)


def _combine_kernel(tot_ref, tab_ref, next_tab_ref, slot_ref, tw_ref, x1_ref, p_ref, gp_ref, wg_ref, wp_ref, gfin_ref,
                    y_ref, o_ref, rows_ref, sems, *, final_norm):
    t = x1_ref.shape[0]
    n_slots = rows_ref.shape[1]
    i = pl.program_id(0)
    cur = lax.rem(i, 2)

    def copy(into, in_tile, in_buf, rows):
        return pltpu.make_async_copy(y_ref.at[pl.ds(in_buf, rows)], rows_ref.at[into, pl.ds(in_tile, rows)],
                                     sems.at[into])

    def start_tile(table, into):
        _for_tile_chunks(table, lambda in_tile, in_buf, rows: copy(into, in_tile, in_buf, rows).start())

    @pl.when(i == 0)
    def _():
        rows_ref[...] = jnp.zeros_like(rows_ref)
        start_tile(tab_ref, 0)

    @pl.when(i + 1 < pl.num_programs(0))
    def _():
        start_tile(next_tab_ref, 1 - cur)

    _wait_rows(tot_ref[i], lambda rows: copy(cur, 0, 0, rows).wait())

    slots = slot_ref[...]
    tw = tw_ref[...]
    slot_id = lax.broadcasted_iota(jnp.int32, (t, n_slots), 1)
    weights = jnp.zeros((t, n_slots), F32)
    for kk in range(TOP_K):
        weights = jnp.where(slot_id == slots[:, kk:kk + 1], tw[:, kk:kk + 1], weights)
    w_hi, w_lo = _split_bf16(weights)
    y = rows_ref[cur].astype(BF16)
    x2 = x1_ref[...] + (_dot(w_hi, y) + _dot(w_lo, y))
    hp = _rms(x2, gp_ref[...]).astype(BF16)
    gate = jax.nn.sigmoid(_dot(hp, wg_ref[...]))
    x3 = x2 + gate * _dot(p_ref[...].astype(BF16), wp_ref[...])
    o_ref[...] = _rms(x3, gfin_ref[...]) if final_norm else x3


def _combine(slots, top_w, x1, p2d, y_buf, tile_rows, chunks, g_ple, w_ple_gate, w_ple_proj, g_final, final_norm, t):
    n, d = x1.shape
    ple = p2d.shape[1]
    n_slots = TOP_K * t + N_EXPERTS * SEG_ALIGN
    last = n // t - 1
    const = lambda *shape: pl.BlockSpec(shape, lambda i, *_: (0,) * len(shape))
    tile = lambda w: pl.BlockSpec((t, w), lambda i, *_: (i, 0))
    table = lambda step: pl.BlockSpec((1, 1, chunks.shape[2]), lambda i, *_: (step(i), 0, 0), memory_space=pltpu.SMEM)
    return pl.pallas_call(
        functools.partial(_combine_kernel, final_norm=final_norm),
        grid_spec=pltpu.PrefetchScalarGridSpec(
            num_scalar_prefetch=1,
            grid=(n // t,),
            in_specs=[table(lambda i: i), table(lambda i: jnp.minimum(i + 1, last)),
                      tile(LANES), tile(LANES), tile(d), tile(ple), const(1, d), const(d, d), const(ple, d),
                      const(1, d), pl.BlockSpec(memory_space=pl.ANY)],
            out_specs=tile(d),
            scratch_shapes=[pltpu.VMEM((2, n_slots, d), F32), pltpu.SemaphoreType.DMA((2,))],
        ),
        out_shape=jax.ShapeDtypeStruct((n, d), F32),
        compiler_params=_params(1),
        name="combine",
    )(tile_rows, chunks, chunks, slots, top_w, x1, p2d, g_ple, w_ple_gate, w_ple_proj, g_final, y_buf)


def _layer(x, p, g_mix, w_in, w_pool_grp, pool_scale, w_pool_up, w_attn_up, w_out, g_ffn, w_router, b_router,
           w_gu, b_gu, w_down, b_down, g_ple, w_ple_gate, w_ple_proj, g_final, final_norm):
    b, s, d = x.shape
    n = b * s
    pool_w = len(POOL_WINDOWS) * POOL_CH
    att_w = N_HEADS * HEAD_DIM
    seq_tile = min(ROW_TILE, s)
    tok_tile = min(ROW_TILE, n)
    sort_tile = min(SORT_TILE, n)
    attn_blk = min(ATTN_BLOCK, s)

    row = lambda v: v.reshape(1, -1)
    w_in = w_in.astype(BF16)
    w_u, w_qkv = w_in[:, :pool_w], w_in[:, pool_w:pool_w + 3 * att_w]
    w_gp, w_ga = w_in[:, pool_w + 3 * att_w:pool_w + 3 * att_w + d], w_in[:, pool_w + 3 * att_w + d:]
    q, k, v, pool_branch, g_att = _mix_in(x, row(g_mix), w_u, w_qkv, w_gp, w_ga, w_pool_grp.astype(BF16),
                                          row(pool_scale), w_pool_up.astype(BF16), seq_tile)
    y_att = _attention(q, k, v, attn_blk)

    wr_hi = w_router.astype(BF16)
    wr_lo = (w_router - wr_hi.astype(F32)).astype(BF16)
    x1, h2, slots, w_top, tile_counts = _mix_out(
        x.reshape(n, d), y_att.reshape(n, att_w), pool_branch.reshape(n, d), g_att.reshape(n, d),
        w_attn_up.astype(BF16), w_out.astype(BF16), row(g_ffn), wr_hi, wr_lo, row(b_router), tok_tile, sort_tile)

    n_tiles = n // sort_tile
    seg_rows = -(-tile_counts[:, 0, :].astype(jnp.int32) // SEG_ALIGN) * SEG_ALIGN
    seg_src = jnp.cumsum(seg_rows, axis=1) - seg_rows
    group_rows = jnp.sum(seg_rows, axis=0)
    padded = (group_rows + EXPERT_BLOCK - 1) // EXPERT_BLOCK * EXPERT_BLOCK
    pad_ends = jnp.cumsum(padded)
    seg_dst = (pad_ends - padded)[None, :] + jnp.cumsum(seg_rows, axis=0) - seg_rows
    n_blocks = -(-(n * TOP_K + n_tiles * N_EXPERTS * (SEG_ALIGN - 1)) // EXPERT_BLOCK) + N_EXPERTS
    block_starts = jnp.arange(n_blocks, dtype=jnp.int32) * EXPERT_BLOCK
    block_e = jnp.minimum(jnp.sum((pad_ends[None, :] <= block_starts[:, None]).astype(jnp.int32), axis=1),
                          N_EXPERTS - 1)
    n_used = (pad_ends[-1:] // EXPERT_BLOCK).astype(jnp.int32)
    tile_rows = jnp.sum(seg_rows, axis=1)
    chunks = _chunk_table(seg_rows, seg_src, seg_dst, sort_tile)

    buf = _dispatch(h2, slots, pad_ends, group_rows, tile_rows, chunks, n_blocks * EXPERT_BLOCK, sort_tile)
    y_buf = _experts(buf, block_e, n_used, w_gu, b_gu[:, None, :], w_down, b_down[:, None, :])
    out = _combine(slots, w_top, x1, p.reshape(n, -1), y_buf, tile_rows, chunks, row(g_ple),
                   w_ple_gate.astype(BF16), w_ple_proj.astype(BF16), row(g_final), final_norm, sort_tile)
    return out.reshape(b, s, d)


def kernel(x, p, g_mix, w_in, w_pool_grp, pool_scale, w_pool_up, w_attn_up, w_out, g_ffn, w_router, b_router,
           w_gu, b_gu, w_down, b_down, g_ple, w_ple_gate, w_ple_proj, g_final):
    depth = p.shape[0]
    for i in range(depth):
        x = _layer(x, p[i], g_mix[i], w_in[i], w_pool_grp[i], pool_scale[i], w_pool_up[i], w_attn_up[i], w_out[i],
                   g_ffn[i], w_router[i], b_router[i], w_gu[i], b_gu[i], w_down[i], b_down[i], g_ple[i],
                   w_ple_gate[i], w_ple_proj[i], g_final, i == depth - 1)
    return x
```

```python
import functools

import jax
import jax.numpy as jnp
from jax import lax
from jax.experimental import pallas as pl
from jax.experimental.pallas import tpu as pltpu

F32 = jnp.float32
BF16 = jnp.bfloat16

N_HEADS = 8
HEAD_DIM = 64
POOL_WINDOWS = (2, 4, 8, 16)
POOL_CH = 128
POOL_HIST = 16
N_EXPERTS = 32
TOP_K = 4
EXPERT_BLOCK = 512
SEG_ALIGN = 8
BIG_CHUNK = 32
WAIT_CHUNK = 256
SMALL_CAP = N_EXPERTS * (BIG_CHUNK // SEG_ALIGN - 1)
SRC_BITS = 12
SWIGLU_LIMIT = 7.0
SWIGLU_ALPHA = 1.702
EPS = 1e-6
LOG2_E = 1.4426950408889634
UNDERFLOW_BITS = 160.0
LANES = 128
ROW_TILE = 512
SORT_TILE = 256
ATTN_BLOCK = 256
VMEM_LIMIT = 56 * 1024 * 1024


def _params(n_axes, flags=None):
    return pltpu.CompilerParams(dimension_semantics=("arbitrary",) * n_axes, vmem_limit_bytes=VMEM_LIMIT,
                                flags=flags)


def _rms(xf, g):
    return xf * lax.rsqrt(jnp.mean(xf * xf, axis=-1, keepdims=True) + EPS) * g


def _dot(a, b):
    return jnp.dot(a, b, preferred_element_type=F32)


def _split_bf16(a):
    hi = a.astype(BF16)
    lo = (a - hi.astype(F32)).astype(BF16)
    return hi, lo


def _mix_in_kernel(x_ref, g_ref, wu_ref, wqkv_ref, wgp_ref, wga_ref, wgrp_ref, pscale_ref, wpu_ref,
                   q_ref, k_ref, v_ref, pb_ref, ga_ref, hist_ref):
    s = pl.program_id(1)
    ts = x_ref.shape[1]
    att_w = q_ref.shape[2]

    @pl.when(s == 0)
    def _():
        hist_ref[...] = jnp.zeros_like(hist_ref)

    h = _rms(x_ref[0], g_ref[...]).astype(BF16)
    u = _dot(h, wu_ref[...])
    ext = jnp.concatenate([hist_ref[...], u], axis=0)
    hist_ref[...] = u[ts - POOL_HIST:, :]
    t_pos = s * ts + lax.broadcasted_iota(jnp.int32, (ts, 1), 0)
    y_groups = []
    for g, w in enumerate(POOL_WINDOWS):
        a = ext[:, g * POOL_CH:(g + 1) * POOL_CH]
        sh = 1
        while sh < w:
            a = a + pltpu.roll(a, sh, axis=0)
            sh *= 2
        count = jnp.minimum(t_pos + 1, w).astype(F32)
        pooled = a[POOL_HIST:, :] / count - u[:, g * POOL_CH:(g + 1) * POOL_CH]
        y_groups.append(_dot(pooled.astype(BF16), wgrp_ref[g]))
    y_pool = jnp.concatenate(y_groups, axis=1) * pscale_ref[...]
    pool_up = _dot(y_pool.astype(BF16), wpu_ref[...])
    g_pool = jax.nn.sigmoid(_dot(h, wgp_ref[...]))
    pb_ref[0] = (g_pool * pool_up).astype(BF16)
    ga_ref[0] = jax.nn.sigmoid(_dot(h, wga_ref[...])).astype(BF16)
    qkv = _dot(h, wqkv_ref[...])
    q_ref[0] = (qkv[:, :att_w] * (HEAD_DIM ** -0.5 * LOG2_E)).astype(BF16)
    k_ref[0] = qkv[:, att_w:2 * att_w].astype(BF16)
    v_ref[0] = qkv[:, 2 * att_w:].astype(BF16)


def _mix_in(x, g_mix, w_u, w_qkv, w_gp, w_ga, w_grp, pool_scale, w_pool_up, ts):
    b, s, d = x.shape
    pool_w = w_u.shape[1]
    att_w = w_qkv.shape[1] // 3
    const = lambda *shape: pl.BlockSpec(shape, lambda i, j: (0,) * len(shape))
    tile = lambda w: pl.BlockSpec((1, ts, w), lambda i, j: (i, j, 0))
    return pl.pallas_call(
        _mix_in_kernel,
        grid=(b, s // ts),
        in_specs=[tile(d), const(1, d), const(d, pool_w), const(d, 3 * att_w), const(d, d), const(d, d),
                  const(len(POOL_WINDOWS), POOL_CH, POOL_CH), const(1, pool_w), const(pool_w, d)],
        out_specs=[tile(att_w), tile(att_w), tile(att_w), tile(d), tile(d)],
        out_shape=[jax.ShapeDtypeStruct((b, s, att_w), BF16)] * 3 + [jax.ShapeDtypeStruct((b, s, d), BF16)] * 2,
        scratch_shapes=[pltpu.VMEM((POOL_HIST, pool_w), F32)],
        compiler_params=_params(2),
        name="mix_in",
    )(x, g_mix, w_u, w_qkv, w_gp, w_ga, w_grp, pool_scale, w_pool_up)


def _attn_kernel(q_ref, k_ref, v_ref, tri_ref, o_ref, *, blk):
    i = pl.program_id(2)
    q2 = q_ref[0]
    lane = lax.broadcasted_iota(jnp.int32, (blk, LANES), 1)
    row = lax.broadcasted_iota(jnp.int32, (blk, blk), 0)
    col = lax.broadcasted_iota(jnp.int32, (blk, blk), 1)
    causal = col < row
    heads = (lane < HEAD_DIM, lane >= HEAD_DIM)
    qs = [jnp.where(hm, q2, jnp.zeros_like(q2)) for hm in heads]

    def keys_values(j):
        start = pl.multiple_of(j * blk, blk)
        return k_ref[0, pl.ds(start, blk), :], v_ref[0, pl.ds(start, blk), :]

    def scores(qh, kj, diagonal):
        z = lax.dot_general(qh, kj, (((1,), (1,)), ((), ())), preferred_element_type=F32)
        softplus = jnp.maximum(z, 0.0) + jnp.log2(1.0 + jnp.exp2(-jnp.abs(z)))
        if diagonal:
            softplus = jnp.where(causal, softplus, 0.0)
        return z, _dot(softplus.astype(BF16), tri_ref[...])

    def accumulate(z, incl, acc, run, vj, diagonal):
        w = jnp.exp2(z - incl - jnp.concatenate([run] * (blk // LANES), axis=1))
        if diagonal:
            w = jnp.where(causal, w, 0.0)
        return acc + _dot(w.astype(BF16), vj), run + jnp.broadcast_to(incl[:, 0:1], (blk, LANES))

    has_prev = i > 0
    kd, vd = keys_values(i)
    kp, vp = keys_values(jnp.maximum(i - 1, 0))
    front_d = [scores(qh, kd, True) for qh in qs]
    front_p = [scores(qh, kp, False) for qh in qs]
    zero = jnp.zeros((blk, LANES), F32)
    accs, runs = [], []
    for (zd, incl_d), (zp, incl_p) in zip(front_d, front_p):
        acc_d, run_d = accumulate(zd, incl_d, zero, zero, vd, True)
        acc_p, run_p = accumulate(zp, incl_p, acc_d, run_d, vp, False)
        accs.append(jnp.where(has_prev, acc_p, acc_d))
        runs.append(jnp.where(has_prev, run_p, run_d))

    def live(c):
        n, _, _, min_run = c
        return jnp.logical_and(n < i, min_run < UNDERFLOW_BITS)

    def step(c):
        n, accs, runs, _ = c
        kj, vj = keys_values(i - 1 - n)
        front = [scores(qh, kj, False) for qh in qs]
        back = [accumulate(z, incl, acc, run, vj, False) for (z, incl), acc, run in zip(front, accs, runs)]
        accs, runs = [a for a, _ in back], [r for _, r in back]
        return n + 1, accs, runs, jnp.minimum(jnp.min(runs[0]), jnp.min(runs[1]))

    min_run = jnp.minimum(jnp.min(runs[0]), jnp.min(runs[1]))
    _, accs, _, _ = lax.while_loop(live, step, (jnp.int32(1), accs, runs, min_run))
    o_ref[0] = jnp.where(heads[0], accs[0], accs[1]).astype(o_ref.dtype)


def _attention(q, k, v, blk):
    b, s, att_w = q.shape
    tri = (jnp.arange(blk)[:, None] >= jnp.arange(blk)[None, :]).astype(BF16)
    qo_spec = pl.BlockSpec((1, blk, LANES), lambda bi, hp, i: (bi, i, hp))
    kv_spec = pl.BlockSpec((1, s, LANES), lambda bi, hp, i: (bi, 0, hp))
    return pl.pallas_call(
        functools.partial(_attn_kernel, blk=blk),
        grid=(b, att_w // LANES, s // blk),
        in_specs=[qo_spec, kv_spec, kv_spec, pl.BlockSpec((blk, blk), lambda bi, hp, i: (0, 0))],
        out_specs=qo_spec,
        out_shape=jax.ShapeDtypeStruct((b, s, att_w), BF16),
        compiler_params=_params(3),
        name="attn",
    )(q, k, v, tri)


def _mix_out_kernel(x_ref, ya_ref, pb_ref, ga_ref, wau_ref, wo_ref, gf_ref, wrh_ref, wrl_ref, br_ref, lt_ref, up_ref,
                    x1_ref, h2_ref, s_ref, w_ref, cnt_ref):
    t = x_ref.shape[0]

    att_up = _dot(ya_ref[...], wau_ref[...])
    merged = pb_ref[...].astype(F32) + ga_ref[...].astype(F32) * att_up
    x1 = x_ref[...] + _dot(merged.astype(BF16), wo_ref[...])
    x1_ref[...] = x1
    h2 = _rms(x1, gf_ref[...])
    h2_ref[...] = h2.astype(h2_ref.dtype)

    hi, lo = _split_bf16(h2)
    logits = _dot(hi, wrh_ref[...]) + _dot(lo, wrh_ref[...]) + _dot(hi, wrl_ref[...]) + br_ref[...]
    lane_e = lax.broadcasted_iota(jnp.int32, (t, N_EXPERTS), 1)
    vals, idxs = [], []
    for _ in range(TOP_K):
        m = jnp.max(logits, axis=-1, keepdims=True)
        idx = jnp.min(jnp.where(logits == m, lane_e, N_EXPERTS), axis=-1, keepdims=True)
        vals.append(m)
        idxs.append(idx)
        logits = jnp.where(lane_e == idx, -jnp.inf, logits)
    exps = [jnp.exp(m - vals[0]) for m in vals]
    denom = exps[0] + exps[1] + exps[2] + exps[3]

    onehot = jnp.zeros((t, N_EXPERTS), F32)
    for idx in idxs:
        onehot = onehot + jnp.where(lane_e == idx, 1.0, 0.0)
    pos = _dot(lt_ref[...], onehot.astype(BF16))
    sort_tile = t // cnt_ref.shape[0]
    seg_starts = []
    for j in range(cnt_ref.shape[0]):
        cnt = jnp.sum(onehot[j * sort_tile:(j + 1) * sort_tile], axis=0, keepdims=True)
        cnt_ref[j] = cnt
        seg_rows = jnp.ceil(cnt * (1.0 / SEG_ALIGN)) * SEG_ALIGN
        before = _dot(jnp.broadcast_to(seg_rows, (SEG_ALIGN, N_EXPERTS)).astype(BF16), up_ref[...])[0:1]
        seg_starts.append(jnp.broadcast_to(before, (sort_tile, N_EXPERTS)))
    pos = pos + jnp.concatenate(seg_starts, axis=0)

    lane = lax.broadcasted_iota(jnp.int32, (t, LANES), 1)
    s_out = jnp.zeros((t, LANES), jnp.int32)
    w_out = jnp.zeros((t, LANES), F32)
    for kk in range(TOP_K):
        slot = jnp.sum(jnp.where(lane_e == idxs[kk], pos, 0.0), axis=-1, keepdims=True).astype(jnp.int32)
        s_out = jnp.where(lane == kk, slot, s_out)
        w_out = jnp.where(lane == kk, exps[kk] / denom, w_out)
    s_ref[...] = s_out
    w_ref[...] = w_out


def _mix_out(x2d, y_att, pool_branch, g_att, w_attn_up, w_out, g_ffn, wr_hi, wr_lo, b_router, t, sort_tile):
    n, d = x2d.shape
    att_w = y_att.shape[1]
    r, c = jnp.arange(t)[:, None], jnp.arange(t)[None, :]
    lower = jnp.logical_and(r > c, r // sort_tile == c // sort_tile).astype(BF16)
    experts = jnp.arange(N_EXPERTS)
    before = (experts[:, None] < experts[None, :]).astype(BF16)
    const = lambda *shape: pl.BlockSpec(shape, lambda i: (0,) * len(shape))
    tile = lambda w: pl.BlockSpec((t, w), lambda i: (i, 0))
    return pl.pallas_call(
        _mix_out_kernel,
        grid=(n // t,),
        in_specs=[tile(d), tile(att_w), tile(d), tile(d), const(att_w, d), const(d, d), const(1, d),
                  const(d, N_EXPERTS), const(d, N_EXPERTS), const(1, N_EXPERTS), const(t, t),
                  const(N_EXPERTS, N_EXPERTS)],
        out_specs=[tile(d), tile(d), tile(LANES), tile(LANES),
                   pl.BlockSpec((t // sort_tile, 1, N_EXPERTS), lambda i: (i, 0, 0))],
        out_shape=[jax.ShapeDtypeStruct((n, d), F32), jax.ShapeDtypeStruct((n, d), BF16),
                   jax.ShapeDtypeStruct((n, LANES), jnp.int32), jax.ShapeDtypeStruct((n, LANES), F32),
                   jax.ShapeDtypeStruct((n // sort_tile, 1, N_EXPERTS), F32)],
        compiler_params=_params(1),
        name="mix_out",
    )(x2d, y_att, pool_branch, g_att, w_attn_up, w_out, g_ffn, wr_hi, wr_lo, b_router, lower, before)


def _chunk_table(seg_rows, seg_src, seg_dst, sort_tile):
    n_big = seg_rows // BIG_CHUNK
    n_small = (seg_rows - n_big * BIG_CHUNK) // SEG_ALIGN

    def copies(counts, first_row, rows, capacity):
        ends = jnp.cumsum(counts, axis=1)
        starts = ends - counts
        idx = jnp.arange(capacity, dtype=jnp.int32)[None, :, None]
        mine = jnp.logical_and(starts[:, None, :] <= idx, idx < ends[:, None, :])
        pick = lambda a: jnp.sum(jnp.where(mine, a[:, None, :], 0), axis=2)
        row = pick(first_row) + (idx[:, :, 0] - pick(starts)) * rows
        return (pick(seg_src) + row) // SEG_ALIGN + ((pick(seg_dst) + row) // SEG_ALIGN << SRC_BITS)

    big_cap = (TOP_K * sort_tile + N_EXPERTS * (SEG_ALIGN - 1)) // BIG_CHUNK
    table = jnp.concatenate([jnp.sum(n_big, axis=1, keepdims=True), jnp.sum(n_small, axis=1, keepdims=True),
                             copies(n_big, jnp.zeros_like(n_big), BIG_CHUNK, big_cap),
                             copies(n_small, n_big * BIG_CHUNK, SEG_ALIGN, SMALL_CAP)], axis=1)
    return table[:, None, :].astype(jnp.int32)


def _for_tile_chunks(tab_ref, act):
    big_cap = tab_ref.shape[2] - 2 - SMALL_CAP

    def run(first, count, rows):
        def one(c):
            packed = tab_ref[0, 0, first + c]
            act(pl.multiple_of(jnp.bitwise_and(packed, (1 << SRC_BITS) - 1) * SEG_ALIGN, SEG_ALIGN),
                pl.multiple_of(lax.shift_right_logical(packed, SRC_BITS) * SEG_ALIGN, SEG_ALIGN), rows)

        def pair(c, carry):
            one(2 * c)
            one(2 * c + 1)
            return carry

        lax.fori_loop(0, lax.shift_right_logical(count, 1), pair, 0)

        @pl.when(jnp.bitwise_and(count, 1) != 0)
        def _():
            one(count - 1)

    run(2, tab_ref[0, 0, 0], BIG_CHUNK)
    run(2 + big_cap, tab_ref[0, 0, 1], SEG_ALIGN)


def _wait_rows(total, wait_chunk):
    n_big = lax.shift_right_logical(total, WAIT_CHUNK.bit_length() - 1)

    def big(j, c):
        wait_chunk(WAIT_CHUNK)
        return c

    lax.fori_loop(0, n_big, big, 0)
    rest = total - n_big * WAIT_CHUNK
    size = WAIT_CHUNK // 2
    while size >= SEG_ALIGN:
        @pl.when(jnp.bitwise_and(rest, size) != 0)
        def _(size=size):
            wait_chunk(size)
        size //= 2


def _dispatch_kernel(pe_ref, cnt_ref, tot_ref, tab_ref, slot_ref, h_ref, buf_ref, zero_ref, sorted_ref, sems, zsem):
    t = h_ref.shape[0]
    n_slots = sorted_ref.shape[1]
    n_blocks = buf_ref.shape[0] // EXPERT_BLOCK
    i = pl.program_id(0)
    cur = lax.rem(i, 2)

    def copy(into, in_tile, in_buf, rows):
        return pltpu.make_async_copy(sorted_ref.at[into, pl.ds(in_tile, rows)], buf_ref.at[pl.ds(in_buf, rows)],
                                     sems.at[into])

    def wait_tile(tile, into):
        _wait_rows(tot_ref[tile], lambda rows: copy(into, 0, 0, rows).wait())

    @pl.when(i == 0)
    def _():
        zero_ref[...] = jnp.zeros_like(zero_ref)

        def zero_copy(start):
            start = pl.multiple_of(start, EXPERT_BLOCK)
            return pltpu.make_async_copy(zero_ref, buf_ref.at[pl.ds(start, EXPERT_BLOCK)], zsem)

        def start_last(e, c):
            @pl.when(cnt_ref[e] > 0)
            def _():
                zero_copy(pe_ref[e] - EXPERT_BLOCK).start()
            return c

        def wait_last(e, c):
            @pl.when(cnt_ref[e] > 0)
            def _():
                zero_copy(pe_ref[e] - EXPERT_BLOCK).wait()
            return c

        def start_tail(b, c):
            zero_copy(b * EXPERT_BLOCK).start()
            return c

        def wait_tail(b, c):
            zero_copy(b * EXPERT_BLOCK).wait()
            return c

        n_used = pe_ref[N_EXPERTS - 1] // EXPERT_BLOCK
        lax.fori_loop(0, N_EXPERTS, start_last, 0)
        lax.fori_loop(n_used, n_blocks, start_tail, 0)
        lax.fori_loop(0, N_EXPERTS, wait_last, 0)
        lax.fori_loop(n_used, n_blocks, wait_tail, 0)

    @pl.when(i >= 2)
    def _():
        wait_tile(i - 2, cur)

    slots = slot_ref[...]
    slot_id = lax.broadcasted_iota(jnp.int32, (t, n_slots), 1)
    hit = slot_id == slots[:, 0:1]
    for kk in range(1, TOP_K):
        hit = jnp.logical_or(hit, slot_id == slots[:, kk:kk + 1])
    select = jnp.where(hit, 1.0, 0.0).astype(BF16)
    sorted_ref[cur] = lax.dot_general(select, h_ref[...], (((0,), (0,)), ((), ())), preferred_element_type=F32)
    _for_tile_chunks(tab_ref, lambda in_tile, in_buf, rows: copy(cur, in_tile, in_buf, rows).start())

    @pl.when(i == pl.num_programs(0) - 1)
    def _():
        @pl.when(i >= 1)
        def _():
            wait_tile(i - 1, 1 - cur)
        wait_tile(i, cur)


def _dispatch(h2, slots, pad_ends, counts, tile_rows, chunks, n_pad, t):
    n, d = h2.shape
    n_slots = TOP_K * t + N_EXPERTS * SEG_ALIGN
    assert n_slots // SEG_ALIGN <= 1 << SRC_BITS and n_pad // SEG_ALIGN < 1 << (31 - SRC_BITS)
    return pl.pallas_call(
        _dispatch_kernel,
        grid_spec=pltpu.PrefetchScalarGridSpec(
            num_scalar_prefetch=3,
            grid=(n // t,),
            in_specs=[pl.BlockSpec((1, 1, chunks.shape[2]), lambda i, *_: (i, 0, 0), memory_space=pltpu.SMEM),
                      pl.BlockSpec((t, LANES), lambda i, *_: (i, 0)), pl.BlockSpec((t, d), lambda i, *_: (i, 0))],
            out_specs=pl.BlockSpec(memory_space=pl.ANY),
            scratch_shapes=[pltpu.VMEM((EXPERT_BLOCK, d), F32), pltpu.VMEM((2, n_slots, d), F32),
                            pltpu.SemaphoreType.DMA((2,)), pltpu.SemaphoreType.DMA],
        ),
        out_shape=jax.ShapeDtypeStruct((n_pad, d), F32),
        compiler_params=_params(1),
        name="dispatch",
    )(pad_ends, counts, tile_rows, chunks, slots, h2)


def _expert_kernel(be_ref, nu_ref, x_ref, wgu_ref, bgu_ref, wd_ref, bd_ref, y_ref, wgu_bf, wd_bf):
    d_exp = wd_ref.shape[1]
    i = pl.program_id(0)

    @pl.when(jnp.logical_or(i == 0, be_ref[i] != be_ref[jnp.maximum(i - 1, 0)]))
    def _():
        wgu_bf[...] = wgu_ref[0].astype(BF16)
        wd_bf[...] = wd_ref[0].astype(BF16)

    @pl.when(i < nu_ref[0])
    def _():
        gu = _dot(x_ref[...].astype(BF16), wgu_bf[...]) + bgu_ref[0]
        gate = jnp.minimum(gu[:, :d_exp], SWIGLU_LIMIT)
        up = jnp.clip(gu[:, d_exp:], -SWIGLU_LIMIT, SWIGLU_LIMIT)
        glu = gate * jax.nn.sigmoid(SWIGLU_ALPHA * gate)
        y_ref[...] = _dot(((up + 1.0) * glu).astype(BF16), wd_bf[...]) + bd_ref[0]

    @pl.when(i >= nu_ref[0])
    def _():
        y_ref[...] = jnp.zeros_like(y_ref)


def _experts(buf, block_e, n_used, w_gu, b_gu, w_down, b_down):
    n_pad, d = buf.shape
    d_exp = w_down.shape[1]
    used = lambda i, be, nu: jnp.minimum(i, nu[0] - 1)
    rows = pl.BlockSpec((EXPERT_BLOCK, d), lambda i, be, nu: (used(i, be, nu), 0))
    out_rows = pl.BlockSpec((EXPERT_BLOCK, d), lambda i, be, nu: (i, 0))
    per_e = lambda *shape: pl.BlockSpec((1,) + shape, lambda i, be, nu: (be[used(i, be, nu)], 0, 0))
    return pl.pallas_call(
        _expert_kernel,
        grid_spec=pltpu.PrefetchScalarGridSpec(
            num_scalar_prefetch=2,
            grid=(n_pad // EXPERT_BLOCK,),
            in_specs=[rows, per_e(d, 2 * d_exp), per_e(1, 2 * d_exp), per_e(d_exp, d), per_e(1, d)],
            out_specs=out_rows,
            scratch_shapes=[pltpu.VMEM((d, 2 * d_exp), BF16), pltpu.VMEM((d_exp, d), BF16)],
        ),
        out_shape=jax.ShapeDtypeStruct((n_pad, d), F32),
        compiler_params=_params(1),
        name="experts",
    )(block_e, n_used, buf, w_gu, b_gu, w_down, b_down)


def _combine_kernel(tot_ref, tab_ref, next_tab_ref, slot_ref, tw_ref, x1_ref, p_ref, gp_ref, wg_ref, wp_ref, gfin_ref,
                    y_ref, o_ref, rows_ref, sems, *, final_norm):
    t = x1_ref.shape[0]
    n_slots = rows_ref.shape[1]
    i = pl.program_id(0)
    cur = lax.rem(i, 2)

    def copy(into, in_tile, in_buf, rows):
        return pltpu.make_async_copy(y_ref.at[pl.ds(in_buf, rows)], rows_ref.at[into, pl.ds(in_tile, rows)],
                                     sems.at[into])

    def start_tile(table, into):
        _for_tile_chunks(table, lambda in_tile, in_buf, rows: copy(into, in_tile, in_buf, rows).start())

    @pl.when(i == 0)
    def _():
        rows_ref[...] = jnp.zeros_like(rows_ref)
        start_tile(tab_ref, 0)

    @pl.when(i + 1 < pl.num_programs(0))
    def _():
        start_tile(next_tab_ref, 1 - cur)

    _wait_rows(tot_ref[i], lambda rows: copy(cur, 0, 0, rows).wait())

    slots = slot_ref[...]
    tw = tw_ref[...]
    slot_id = lax.broadcasted_iota(jnp.int32, (t, n_slots), 1)
    weights = jnp.zeros((t, n_slots), F32)
    for kk in range(TOP_K):
        weights = jnp.where(slot_id == slots[:, kk:kk + 1], tw[:, kk:kk + 1], weights)
    w_hi, w_lo = _split_bf16(weights)
    y = rows_ref[cur].astype(BF16)
    x2 = x1_ref[...] + (_dot(w_hi, y) + _dot(w_lo, y))
    hp = _rms(x2, gp_ref[...]).astype(BF16)
    gate = jax.nn.sigmoid(_dot(hp, wg_ref[...]))
    x3 = x2 + gate * _dot(p_ref[...].astype(BF16), wp_ref[...])
    o_ref[...] = _rms(x3, gfin_ref[...]) if final_norm else x3


def _combine(slots, top_w, x1, p2d, y_buf, tile_rows, chunks, g_ple, w_ple_gate, w_ple_proj, g_final, final_norm, t):
    n, d = x1.shape
    ple = p2d.shape[1]
    n_slots = TOP_K * t + N_EXPERTS * SEG_ALIGN
    last = n // t - 1
    const = lambda *shape: pl.BlockSpec(shape, lambda i, *_: (0,) * len(shape))
    tile = lambda w: pl.BlockSpec((t, w), lambda i, *_: (i, 0))
    table = lambda step: pl.BlockSpec((1, 1, chunks.shape[2]), lambda i, *_: (step(i), 0, 0), memory_space=pltpu.SMEM)
    return pl.pallas_call(
        functools.partial(_combine_kernel, final_norm=final_norm),
        grid_spec=pltpu.PrefetchScalarGridSpec(
            num_scalar_prefetch=1,
            grid=(n // t,),
            in_specs=[table(lambda i: i), table(lambda i: jnp.minimum(i + 1, last)),
                      tile(LANES), tile(LANES), tile(d), tile(ple), const(1, d), const(d, d), const(ple, d),
                      const(1, d), pl.BlockSpec(memory_space=pl.ANY)],
            out_specs=tile(d),
            scratch_shapes=[pltpu.VMEM((2, n_slots, d), F32), pltpu.SemaphoreType.DMA((2,))],
        ),
        out_shape=jax.ShapeDtypeStruct((n, d), F32),
        compiler_params=_params(1),
        name="combine",
    )(tile_rows, chunks, chunks, slots, top_w, x1, p2d, g_ple, w_ple_gate, w_ple_proj, g_final, y_buf)


def _layer(x, p, g_mix, w_in, w_pool_grp, pool_scale, w_pool_up, w_attn_up, w_out, g_ffn, w_router, b_router,
           w_gu, b_gu, w_down, b_down, g_ple, w_ple_gate, w_ple_proj, g_final, final_norm):
    b, s, d = x.shape
    n = b * s
    pool_w = len(POOL_WINDOWS) * POOL_CH
    att_w = N_HEADS * HEAD_DIM
    seq_tile = min(ROW_TILE, s)
    tok_tile = min(ROW_TILE, n)
    sort_tile = min(SORT_TILE, n)
    attn_blk = min(ATTN_BLOCK, s)

    row = lambda v: v.reshape(1, -1)
    w_in = w_in.astype(BF16)
    w_u, w_qkv = w_in[:, :pool_w], w_in[:, pool_w:pool_w + 3 * att_w]
    w_gp, w_ga = w_in[:, pool_w + 3 * att_w:pool_w + 3 * att_w + d], w_in[:, pool_w + 3 * att_w + d:]
    q, k, v, pool_branch, g_att = _mix_in(x, row(g_mix), w_u, w_qkv, w_gp, w_ga, w_pool_grp.astype(BF16),
                                          row(pool_scale), w_pool_up.astype(BF16), seq_tile)
    y_att = _attention(q, k, v, attn_blk)

    wr_hi = w_router.astype(BF16)
    wr_lo = (w_router - wr_hi.astype(F32)).astype(BF16)
    x1, h2, slots, w_top, tile_counts = _mix_out(
        x.reshape(n, d), y_att.reshape(n, att_w), pool_branch.reshape(n, d), g_att.reshape(n, d),
        w_attn_up.astype(BF16), w_out.astype(BF16), row(g_ffn), wr_hi, wr_lo, row(b_router), tok_tile, sort_tile)

    n_tiles = n // sort_tile
    seg_rows = -(-tile_counts[:, 0, :].astype(jnp.int32) // SEG_ALIGN) * SEG_ALIGN
    seg_src = jnp.cumsum(seg_rows, axis=1) - seg_rows
    group_rows = jnp.sum(seg_rows, axis=0)
    padded = (group_rows + EXPERT_BLOCK - 1) // EXPERT_BLOCK * EXPERT_BLOCK
    pad_ends = jnp.cumsum(padded)
    seg_dst = (pad_ends - padded)[None, :] + jnp.cumsum(seg_rows, axis=0) - seg_rows
    n_blocks = -(-(n * TOP_K + n_tiles * N_EXPERTS * (SEG_ALIGN - 1)) // EXPERT_BLOCK) + N_EXPERTS
    block_starts = jnp.arange(n_blocks, dtype=jnp.int32) * EXPERT_BLOCK
    block_e = jnp.minimum(jnp.sum((pad_ends[None, :] <= block_starts[:, None]).astype(jnp.int32), axis=1),
                          N_EXPERTS - 1)
    n_used = (pad_ends[-1:] // EXPERT_BLOCK).astype(jnp.int32)
    tile_rows = jnp.sum(seg_rows, axis=1)
    chunks = _chunk_table(seg_rows, seg_src, seg_dst, sort_tile)

    buf = _dispatch(h2, slots, pad_ends, group_rows, tile_rows, chunks, n_blocks * EXPERT_BLOCK, sort_tile)
    y_buf = _experts(buf, block_e, n_used, w_gu, b_gu[:, None, :], w_down, b_down[:, None, :])
    out = _combine(slots, w_top, x1, p.reshape(n, -1), y_buf, tile_rows, chunks, row(g_ple),
                   w_ple_gate.astype(BF16), w_ple_proj.astype(BF16), row(g_final), final_norm, sort_tile)
    return out.reshape(b, s, d)


def kernel(x, p, g_mix, w_in, w_pool_grp, pool_scale, w_pool_up, w_attn_up, w_out, g_ffn, w_router, b_router,
           w_gu, b_gu, w_down, b_down, g_ple, w_ple_gate, w_ple_proj, g_final):
    depth = p.shape[0]
    for i in range(depth):
        x = _layer(x, p[i], g_mix[i], w_in[i], w_pool_grp[i], pool_scale[i], w_pool_up[i], w_attn_up[i], w_out[i],
                   g_ffn[i], w_router[i], b_router[i], w_gu[i], b_gu[i], w_down[i], b_down[i], g_ple[i],
                   w_ple_gate[i], w_ple_proj[i], g_final, i == depth - 1)
    return x
```

```python
import functools

import jax
import jax.numpy as jnp
from jax import lax
from jax.experimental import pallas as pl
from jax.experimental.pallas import tpu as pltpu

F32 = jnp.float32
BF16 = jnp.bfloat16

N_HEADS = 8
HEAD_DIM = 64
POOL_WINDOWS = (2, 4, 8, 16)
POOL_CH = 128
POOL_HIST = 16
N_EXPERTS = 32
TOP_K = 4
EXPERT_BLOCK = 512
SEG_ALIGN = 8
BIG_CHUNK = 32
WAIT_CHUNK = 256
SMALL_CAP = N_EXPERTS * (BIG_CHUNK // SEG_ALIGN - 1)
SRC_BITS = 12
SWIGLU_LIMIT = 7.0
SWIGLU_ALPHA = 1.702
EPS = 1e-6
LOG2_E = 1.4426950408889634
UNDERFLOW_BITS = 160.0
LANES = 128
ROW_TILE = 512
SORT_TILE = 256
ATTN_BLOCK = 256
VMEM_LIMIT = 56 * 1024 * 1024


def _params(n_axes, flags=None):
    return pltpu.CompilerParams(dimension_semantics=("arbitrary",) * n_axes, vmem_limit_bytes=VMEM_LIMIT,
                                flags=flags)


def _rms(xf, g):
    return xf * lax.rsqrt(jnp.mean(xf * xf, axis=-1, keepdims=True) + EPS) * g


def _dot(a, b):
    return jnp.dot(a, b, preferred_element_type=F32)


def _split_bf16(a):
    hi = a.astype(BF16)
    lo = (a - hi.astype(F32)).astype(BF16)
    return hi, lo


def _mix_in_kernel(x_ref, g_ref, wu_ref, wqkv_ref, wgp_ref, wga_ref, wgrp_ref, pscale_ref, wpu_ref,
                   q_ref, k_ref, v_ref, pb_ref, ga_ref, hist_ref):
    s = pl.program_id(1)
    ts = x_ref.shape[1]
    att_w = q_ref.shape[2]

    @pl.when(s == 0)
    def _():
        hist_ref[...] = jnp.zeros_like(hist_ref)

    h = _rms(x_ref[0], g_ref[...]).astype(BF16)
    u = _dot(h, wu_ref[...])
    ext = jnp.concatenate([hist_ref[...], u], axis=0)
    hist_ref[...] = u[ts - POOL_HIST:, :]
    t_pos = s * ts + lax.broadcasted_iota(jnp.int32, (ts, 1), 0)
    y_groups = []
    for g, w in enumerate(POOL_WINDOWS):
        a = ext[:, g * POOL_CH:(g + 1) * POOL_CH]
        sh = 1
        while sh < w:
            a = a + pltpu.roll(a, sh, axis=0)
            sh *= 2
        count = jnp.minimum(t_pos + 1, w).astype(F32)
        pooled = a[POOL_HIST:, :] / count - u[:, g * POOL_CH:(g + 1) * POOL_CH]
        y_groups.append(_dot(pooled.astype(BF16), wgrp_ref[g]))
    y_pool = jnp.concatenate(y_groups, axis=1) * pscale_ref[...]
    pool_up = _dot(y_pool.astype(BF16), wpu_ref[...])
    g_pool = jax.nn.sigmoid(_dot(h, wgp_ref[...]))
    pb_ref[0] = (g_pool * pool_up).astype(BF16)
    ga_ref[0] = jax.nn.sigmoid(_dot(h, wga_ref[...])).astype(BF16)
    qkv = _dot(h, wqkv_ref[...])
    q_ref[0] = (qkv[:, :att_w] * (HEAD_DIM ** -0.5 * LOG2_E)).astype(BF16)
    k_ref[0] = qkv[:, att_w:2 * att_w].astype(BF16)
    v_ref[0] = qkv[:, 2 * att_w:].astype(BF16)


def _mix_in(x, g_mix, w_u, w_qkv, w_gp, w_ga, w_grp, pool_scale, w_pool_up, ts):
    b, s, d = x.shape
    pool_w = w_u.shape[1]
    att_w = w_qkv.shape[1] // 3
    const = lambda *shape: pl.BlockSpec(shape, lambda i, j: (0,) * len(shape))
    tile = lambda w: pl.BlockSpec((1, ts, w), lambda i, j: (i, j, 0))
    return pl.pallas_call(
        _mix_in_kernel,
        grid=(b, s // ts),
        in_specs=[tile(d), const(1, d), const(d, pool_w), const(d, 3 * att_w), const(d, d), const(d, d),
                  const(len(POOL_WINDOWS), POOL_CH, POOL_CH), const(1, pool_w), const(pool_w, d)],
        out_specs=[tile(att_w), tile(att_w), tile(att_w), tile(d), tile(d)],
        out_shape=[jax.ShapeDtypeStruct((b, s, att_w), BF16)] * 3 + [jax.ShapeDtypeStruct((b, s, d), BF16)] * 2,
        scratch_shapes=[pltpu.VMEM((POOL_HIST, pool_w), F32)],
        compiler_params=_params(2),
        name="mix_in",
    )(x, g_mix, w_u, w_qkv, w_gp, w_ga, w_grp, pool_scale, w_pool_up)


def _attn_kernel(q_ref, k_ref, v_ref, tri_ref, o_ref, *, blk):
    i = pl.program_id(2)
    q2 = q_ref[0]
    lane = lax.broadcasted_iota(jnp.int32, (blk, LANES), 1)
    row = lax.broadcasted_iota(jnp.int32, (blk, blk), 0)
    col = lax.broadcasted_iota(jnp.int32, (blk, blk), 1)
    causal = col < row
    heads = (lane < HEAD_DIM, lane >= HEAD_DIM)
    qs = [jnp.where(hm, q2, jnp.zeros_like(q2)) for hm in heads]

    def keys_values(j):
        start = pl.multiple_of(j * blk, blk)
        return k_ref[0, pl.ds(start, blk), :], v_ref[0, pl.ds(start, blk), :]

    def scores(qh, kj, diagonal):
        z = lax.dot_general(qh, kj, (((1,), (1,)), ((), ())), preferred_element_type=F32)
        softplus = jnp.maximum(z, 0.0) + jnp.log2(1.0 + jnp.exp2(-jnp.abs(z)))
        if diagonal:
            softplus = jnp.where(causal, softplus, 0.0)
        return z, _dot(softplus.astype(BF16), tri_ref[...])

    def accumulate(z, incl, acc, run, vj, diagonal):
        w = jnp.exp2(z - incl - jnp.concatenate([run] * (blk // LANES), axis=1))
        if diagonal:
            w = jnp.where(causal, w, 0.0)
        return acc + _dot(w.astype(BF16), vj), run + jnp.broadcast_to(incl[:, 0:1], (blk, LANES))

    has_prev = i > 0
    kd, vd = keys_values(i)
    kp, vp = keys_values(jnp.maximum(i - 1, 0))
    front_d = [scores(qh, kd, True) for qh in qs]
    front_p = [scores(qh, kp, False) for qh in qs]
    zero = jnp.zeros((blk, LANES), F32)
    accs, runs = [], []
    for (zd, incl_d), (zp, incl_p) in zip(front_d, front_p):
        acc_d, run_d = accumulate(zd, incl_d, zero, zero, vd, True)
        acc_p, run_p = accumulate(zp, incl_p, acc_d, run_d, vp, False)
        accs.append(jnp.where(has_prev, acc_p, acc_d))
        runs.append(jnp.where(has_prev, run_p, run_d))

    def live(c):
        n, _, _, min_run = c
        return jnp.logical_and(n < i, min_run < UNDERFLOW_BITS)

    def step(c):
        n, accs, runs, _ = c
        kj, vj = keys_values(i - 1 - n)
        front = [scores(qh, kj, False) for qh in qs]
        back = [accumulate(z, incl, acc, run, vj, False) for (z, incl), acc, run in zip(front, accs, runs)]
        accs, runs = [a for a, _ in back], [r for _, r in back]
        return n + 1, accs, runs, jnp.minimum(jnp.min(runs[0]), jnp.min(runs[1]))

    min_run = jnp.minimum(jnp.min(runs[0]), jnp.min(runs[1]))
    _, accs, _, _ = lax.while_loop(live, step, (jnp.int32(1), accs, runs, min_run))
    o_ref[0] = jnp.where(heads[0], accs[0], accs[1]).astype(o_ref.dtype)


def _attention(q, k, v, blk):
    b, s, att_w = q.shape
    tri = (jnp.arange(blk)[:, None] >= jnp.arange(blk)[None, :]).astype(BF16)
    qo_spec = pl.BlockSpec((1, blk, LANES), lambda bi, hp, i: (bi, i, hp))
    kv_spec = pl.BlockSpec((1, s, LANES), lambda bi, hp, i: (bi, 0, hp))
    return pl.pallas_call(
        functools.partial(_attn_kernel, blk=blk),
        grid=(b, att_w // LANES, s // blk),
        in_specs=[qo_spec, kv_spec, kv_spec, pl.BlockSpec((blk, blk), lambda bi, hp, i: (0, 0))],
        out_specs=qo_spec,
        out_shape=jax.ShapeDtypeStruct((b, s, att_w), BF16),
        compiler_params=_params(3),
        name="attn",
    )(q, k, v, tri)


def _mix_out_kernel(x_ref, ya_ref, pb_ref, ga_ref, wau_ref, wo_ref, gf_ref, wrh_ref, wrl_ref, br_ref, lt_ref, up_ref,
                    x1_ref, h2_ref, s_ref, w_ref, cnt_ref):
    t = x_ref.shape[0]

    att_up = _dot(ya_ref[...], wau_ref[...])
    merged = pb_ref[...].astype(F32) + ga_ref[...].astype(F32) * att_up
    x1 = x_ref[...] + _dot(merged.astype(BF16), wo_ref[...])
    x1_ref[...] = x1
    h2 = _rms(x1, gf_ref[...])
    h2_ref[...] = h2.astype(h2_ref.dtype)

    hi, lo = _split_bf16(h2)
    logits = _dot(hi, wrh_ref[...]) + _dot(lo, wrh_ref[...]) + _dot(hi, wrl_ref[...]) + br_ref[...]
    lane_e = lax.broadcasted_iota(jnp.int32, (t, N_EXPERTS), 1)
    vals, idxs = [], []
    for _ in range(TOP_K):
        m = jnp.max(logits, axis=-1, keepdims=True)
        idx = jnp.min(jnp.where(logits == m, lane_e, N_EXPERTS), axis=-1, keepdims=True)
        vals.append(m)
        idxs.append(idx)
        logits = jnp.where(lane_e == idx, -jnp.inf, logits)
    exps = [jnp.exp(m - vals[0]) for m in vals]
    denom = exps[0] + exps[1] + exps[2] + exps[3]

    onehot = jnp.zeros((t, N_EXPERTS), F32)
    for idx in idxs:
        onehot = onehot + jnp.where(lane_e == idx, 1.0, 0.0)
    pos = _dot(lt_ref[...], onehot.astype(BF16))
    sort_tile = t // cnt_ref.shape[0]
    seg_starts = []
    for j in range(cnt_ref.shape[0]):
        cnt = jnp.sum(onehot[j * sort_tile:(j + 1) * sort_tile], axis=0, keepdims=True)
        cnt_ref[j] = cnt
        seg_rows = jnp.ceil(cnt * (1.0 / SEG_ALIGN)) * SEG_ALIGN
        before = _dot(jnp.broadcast_to(seg_rows, (SEG_ALIGN, N_EXPERTS)).astype(BF16), up_ref[...])[0:1]
        seg_starts.append(jnp.broadcast_to(before, (sort_tile, N_EXPERTS)))
    pos = pos + jnp.concatenate(seg_starts, axis=0)

    lane = lax.broadcasted_iota(jnp.int32, (t, LANES), 1)
    s_out = jnp.zeros((t, LANES), jnp.int32)
    w_out = jnp.zeros((t, LANES), F32)
    for kk in range(TOP_K):
        slot = jnp.sum(jnp.where(lane_e == idxs[kk], pos, 0.0), axis=-1, keepdims=True).astype(jnp.int32)
        s_out = jnp.where(lane == kk, slot, s_out)
        w_out = jnp.where(lane == kk, exps[kk] / denom, w_out)
    s_ref[...] = s_out
    w_ref[...] = w_out


def _mix_out(x2d, y_att, pool_branch, g_att, w_attn_up, w_out, g_ffn, wr_hi, wr_lo, b_router, t, sort_tile):
    n, d = x2d.shape
    att_w = y_att.shape[1]
    r, c = jnp.arange(t)[:, None], jnp.arange(t)[None, :]
    lower = jnp.logical_and(r > c, r // sort_tile == c // sort_tile).astype(BF16)
    experts = jnp.arange(N_EXPERTS)
    before = (experts[:, None] < experts[None, :]).astype(BF16)
    const = lambda *shape: pl.BlockSpec(shape, lambda i: (0,) * len(shape))
    tile = lambda w: pl.BlockSpec((t, w), lambda i: (i, 0))
    return pl.pallas_call(
        _mix_out_kernel,
        grid=(n // t,),
        in_specs=[tile(d), tile(att_w), tile(d), tile(d), const(att_w, d), const(d, d), const(1, d),
                  const(d, N_EXPERTS), const(d, N_EXPERTS), const(1, N_EXPERTS), const(t, t),
                  const(N_EXPERTS, N_EXPERTS)],
        out_specs=[tile(d), tile(d), tile(LANES), tile(LANES),
                   pl.BlockSpec((t // sort_tile, 1, N_EXPERTS), lambda i: (i, 0, 0))],
        out_shape=[jax.ShapeDtypeStruct((n, d), F32), jax.ShapeDtypeStruct((n, d), BF16),
                   jax.ShapeDtypeStruct((n, LANES), jnp.int32), jax.ShapeDtypeStruct((n, LANES), F32),
                   jax.ShapeDtypeStruct((n // sort_tile, 1, N_EXPERTS), F32)],
        compiler_params=_params(1),
        name="mix_out",
    )(x2d, y_att, pool_branch, g_att, w_attn_up, w_out, g_ffn, wr_hi, wr_lo, b_router, lower, before)


def _chunk_table(seg_rows, seg_src, seg_dst, sort_tile):
    n_big = seg_rows // BIG_CHUNK
    n_small = (seg_rows - n_big * BIG_CHUNK) // SEG_ALIGN

    def copies(counts, first_row, rows, capacity):
        ends = jnp.cumsum(counts, axis=1)
        starts = ends - counts
        idx = jnp.arange(capacity, dtype=jnp.int32)[None, :, None]
        mine = jnp.logical_and(starts[:, None, :] <= idx, idx < ends[:, None, :])
        pick = lambda a: jnp.sum(jnp.where(mine, a[:, None, :], 0), axis=2)
        row = pick(first_row) + (idx[:, :, 0] - pick(starts)) * rows
        return (pick(seg_src) + row) // SEG_ALIGN + ((pick(seg_dst) + row) // SEG_ALIGN << SRC_BITS)

    big_cap = (TOP_K * sort_tile + N_EXPERTS * (SEG_ALIGN - 1)) // BIG_CHUNK
    table = jnp.concatenate([jnp.sum(n_big, axis=1, keepdims=True), jnp.sum(n_small, axis=1, keepdims=True),
                             copies(n_big, jnp.zeros_like(n_big), BIG_CHUNK, big_cap),
                             copies(n_small, n_big * BIG_CHUNK, SEG_ALIGN, SMALL_CAP)], axis=1)
    return table[:, None, :].astype(jnp.int32)


def _for_tile_chunks(tab_ref, act):
    big_cap = tab_ref.shape[2] - 2 - SMALL_CAP

    def run(first, count, rows):
        def one(c):
            packed = tab_ref[0, 0, first + c]
            act(pl.multiple_of(jnp.bitwise_and(packed, (1 << SRC_BITS) - 1) * SEG_ALIGN, SEG_ALIGN),
                pl.multiple_of(lax.shift_right_logical(packed, SRC_BITS) * SEG_ALIGN, SEG_ALIGN), rows)

        def pair(c, carry):
            one(2 * c)
            one(2 * c + 1)
            return carry

        lax.fori_loop(0, lax.shift_right_logical(count, 1), pair, 0)

        @pl.when(jnp.bitwise_and(count, 1) != 0)
        def _():
            one(count - 1)

    run(2, tab_ref[0, 0, 0], BIG_CHUNK)
    run(2 + big_cap, tab_ref[0, 0, 1], SEG_ALIGN)


def _wait_rows(total, wait_chunk):
    n_big = lax.shift_right_logical(total, WAIT_CHUNK.bit_length() - 1)

    def big(j, c):
        wait_chunk(WAIT_CHUNK)
        return c

    lax.fori_loop(0, n_big, big, 0)
    rest = total - n_big * WAIT_CHUNK
    size = WAIT_CHUNK // 2
    while size >= SEG_ALIGN:
        @pl.when(jnp.bitwise_and(rest, size) != 0)
        def _(size=size):
            wait_chunk(size)
        size //= 2


def _dispatch_kernel(pe_ref, cnt_ref, tot_ref, tab_ref, slot_ref, h_ref, buf_ref, zero_ref, sorted_ref, sems, zsem):
    t = h_ref.shape[0]
    n_slots = sorted_ref.shape[1]
    n_blocks = buf_ref.shape[0] // EXPERT_BLOCK
    i = pl.program_id(0)
    cur = lax.rem(i, 2)

    def copy(into, in_tile, in_buf, rows):
        return pltpu.make_async_copy(sorted_ref.at[into, pl.ds(in_tile, rows)], buf_ref.at[pl.ds(in_buf, rows)],
                                     sems.at[into])

    def wait_tile(tile, into):
        _wait_rows(tot_ref[tile], lambda rows: copy(into, 0, 0, rows).wait())

    @pl.when(i == 0)
    def _():
        zero_ref[...] = jnp.zeros_like(zero_ref)

        def zero_copy(start):
            start = pl.multiple_of(start, EXPERT_BLOCK)
            return pltpu.make_async_copy(zero_ref, buf_ref.at[pl.ds(start, EXPERT_BLOCK)], zsem)

        def start_last(e, c):
            @pl.when(cnt_ref[e] > 0)
            def _():
                zero_copy(pe_ref[e] - EXPERT_BLOCK).start()
            return c

        def wait_last(e, c):
            @pl.when(cnt_ref[e] > 0)
            def _():
                zero_copy(pe_ref[e] - EXPERT_BLOCK).wait()
            return c

        def start_tail(b, c):
            zero_copy(b * EXPERT_BLOCK).start()
            return c

        def wait_tail(b, c):
            zero_copy(b * EXPERT_BLOCK).wait()
            return c

        n_used = pe_ref[N_EXPERTS - 1] // EXPERT_BLOCK
        lax.fori_loop(0, N_EXPERTS, start_last, 0)
        lax.fori_loop(n_used, n_blocks, start_tail, 0)
        lax.fori_loop(0, N_EXPERTS, wait_last, 0)
        lax.fori_loop(n_used, n_blocks, wait_tail, 0)

    @pl.when(i >= 2)
    def _():
        wait_tile(i - 2, cur)

    slots = slot_ref[...]
    slot_id = lax.broadcasted_iota(jnp.int32, (t, n_slots), 1)
    select = jnp.zeros((t, n_slots), F32)
    for kk in range(TOP_K):
        select = jnp.where(slot_id == slots[:, kk:kk + 1], 1.0, select)
    select = select.astype(BF16)
    sorted_ref[cur] = lax.dot_general(select, h_ref[...], (((0,), (0,)), ((), ())), preferred_element_type=F32)
    _for_tile_chunks(tab_ref, lambda in_tile, in_buf, rows: copy(cur, in_tile, in_buf, rows).start())

    @pl.when(i == pl.num_programs(0) - 1)
    def _():
        @pl.when(i >= 1)
        def _():
            wait_tile(i - 1, 1 - cur)
        wait_tile(i, cur)


def _dispatch(h2, slots, pad_ends, counts, tile_rows, chunks, n_pad, t):
    n, d = h2.shape
    n_slots = TOP_K * t + N_EXPERTS * SEG_ALIGN
    assert n_slots // SEG_ALIGN <= 1 << SRC_BITS and n_pad // SEG_ALIGN < 1 << (31 - SRC_BITS)
    return pl.pallas_call(
        _dispatch_kernel,
        grid_spec=pltpu.PrefetchScalarGridSpec(
            num_scalar_prefetch=3,
            grid=(n // t,),
            in_specs=[pl.BlockSpec((1, 1, chunks.shape[2]), lambda i, *_: (i, 0, 0), memory_space=pltpu.SMEM),
                      pl.BlockSpec((t, LANES), lambda i, *_: (i, 0)), pl.BlockSpec((t, d), lambda i, *_: (i, 0))],
            out_specs=pl.BlockSpec(memory_space=pl.ANY),
            scratch_shapes=[pltpu.VMEM((EXPERT_BLOCK, d), F32), pltpu.VMEM((2, n_slots, d), F32),
                            pltpu.SemaphoreType.DMA((2,)), pltpu.SemaphoreType.DMA],
        ),
        out_shape=jax.ShapeDtypeStruct((n_pad, d), F32),
        compiler_params=_params(1),
        name="dispatch",
    )(pad_ends, counts, tile_rows, chunks, slots, h2)


def _expert_kernel(be_ref, nu_ref, x_ref, wgu_ref, bgu_ref, wd_ref, bd_ref, y_ref, wgu_bf, wd_bf):
    d_exp = wd_ref.shape[1]
    i = pl.program_id(0)

    @pl.when(jnp.logical_or(i == 0, be_ref[i] != be_ref[jnp.maximum(i - 1, 0)]))
    def _():
        wgu_bf[...] = wgu_ref[0].astype(BF16)
        wd_bf[...] = wd_ref[0].astype(BF16)

    @pl.when(i < nu_ref[0])
    def _():
        gu = _dot(x_ref[...].astype(BF16), wgu_bf[...]) + bgu_ref[0]
        gate = jnp.minimum(gu[:, :d_exp], SWIGLU_LIMIT)
        up = jnp.clip(gu[:, d_exp:], -SWIGLU_LIMIT, SWIGLU_LIMIT)
        glu = gate * jax.nn.sigmoid(SWIGLU_ALPHA * gate)
        y_ref[...] = _dot(((up + 1.0) * glu).astype(BF16), wd_bf[...]) + bd_ref[0]

    @pl.when(i >= nu_ref[0])
    def _():
        y_ref[...] = jnp.zeros_like(y_ref)


def _experts(buf, block_e, n_used, w_gu, b_gu, w_down, b_down):
    n_pad, d = buf.shape
    d_exp = w_down.shape[1]
    used = lambda i, be, nu: jnp.minimum(i, nu[0] - 1)
    rows = pl.BlockSpec((EXPERT_BLOCK, d), lambda i, be, nu: (used(i, be, nu), 0))
    out_rows = pl.BlockSpec((EXPERT_BLOCK, d), lambda i, be, nu: (i, 0))
    per_e = lambda *shape: pl.BlockSpec((1,) + shape, lambda i, be, nu: (be[used(i, be, nu)], 0, 0))
    return pl.pallas_call(
        _expert_kernel,
        grid_spec=pltpu.PrefetchScalarGridSpec(
            num_scalar_prefetch=2,
            grid=(n_pad // EXPERT_BLOCK,),
            in_specs=[rows, per_e(d, 2 * d_exp), per_e(1, 2 * d_exp), per_e(d_exp, d), per_e(1, d)],
            out_specs=out_rows,
            scratch_shapes=[pltpu.VMEM((d, 2 * d_exp), BF16), pltpu.VMEM((d_exp, d), BF16)],
        ),
        out_shape=jax.ShapeDtypeStruct((n_pad, d), F32),
        compiler_params=_params(1),
        name="experts",
    )(block_e, n_used, buf, w_gu, b_gu, w_down, b_down)


def _combine_kernel(tot_ref, tab_ref, next_tab_ref, slot_ref, tw_ref, x1_ref, p_ref, gp_ref, wg_ref, wp_ref, gfin_ref,
                    y_ref, o_ref, rows_ref, sems, *, final_norm):
    t = x1_ref.shape[0]
    n_slots = rows_ref.shape[1]
    i = pl.program_id(0)
    cur = lax.rem(i, 2)

    def copy(into, in_tile, in_buf, rows):
        return pltpu.make_async_copy(y_ref.at[pl.ds(in_buf, rows)], rows_ref.at[into, pl.ds(in_tile, rows)],
                                     sems.at[into])

    def start_tile(table, into):
        _for_tile_chunks(table, lambda in_tile, in_buf, rows: copy(into, in_tile, in_buf, rows).start())

    @pl.when(i == 0)
    def _():
        rows_ref[...] = jnp.zeros_like(rows_ref)
        start_tile(tab_ref, 0)

    @pl.when(i + 1 < pl.num_programs(0))
    def _():
        start_tile(next_tab_ref, 1 - cur)

    _wait_rows(tot_ref[i], lambda rows: copy(cur, 0, 0, rows).wait())

    slots = slot_ref[...]
    tw = tw_ref[...]
    slot_id = lax.broadcasted_iota(jnp.int32, (t, n_slots), 1)
    weights = jnp.zeros((t, n_slots), F32)
    for kk in range(TOP_K):
        weights = jnp.where(slot_id == slots[:, kk:kk + 1], tw[:, kk:kk + 1], weights)
    w_hi, w_lo = _split_bf16(weights)
    y = rows_ref[cur].astype(BF16)
    x2 = x1_ref[...] + (_dot(w_hi, y) + _dot(w_lo, y))
    hp = _rms(x2, gp_ref[...]).astype(BF16)
    gate = jax.nn.sigmoid(_dot(hp, wg_ref[...]))
    x3 = x2 + gate * _dot(p_ref[...].astype(BF16), wp_ref[...])
    o_ref[...] = _rms(x3, gfin_ref[...]) if final_norm else x3


def _combine(slots, top_w, x1, p2d, y_buf, tile_rows, chunks, g_ple, w_ple_gate, w_ple_proj, g_final, final_norm, t):
    n, d = x1.shape
    ple = p2d.shape[1]
    n_slots = TOP_K * t + N_EXPERTS * SEG_ALIGN
    last = n // t - 1
    const = lambda *shape: pl.BlockSpec(shape, lambda i, *_: (0,) * len(shape))
    tile = lambda w: pl.BlockSpec((t, w), lambda i, *_: (i, 0))
    table = lambda step: pl.BlockSpec((1, 1, chunks.shape[2]), lambda i, *_: (step(i), 0, 0), memory_space=pltpu.SMEM)
    return pl.pallas_call(
        functools.partial(_combine_kernel, final_norm=final_norm),
        grid_spec=pltpu.PrefetchScalarGridSpec(
            num_scalar_prefetch=1,
            grid=(n // t,),
            in_specs=[table(lambda i: i), table(lambda i: jnp.minimum(i + 1, last)),
                      tile(LANES), tile(LANES), tile(d), tile(ple), const(1, d), const(d, d), const(ple, d),
                      const(1, d), pl.BlockSpec(memory_space=pl.ANY)],
            out_specs=tile(d),
            scratch_shapes=[pltpu.VMEM((2, n_slots, d), F32), pltpu.SemaphoreType.DMA((2,))],
        ),
        out_shape=jax.ShapeDtypeStruct((n, d), F32),
        compiler_params=_params(1),
        name="combine",
    )(tile_rows, chunks, chunks, slots, top_w, x1, p2d, g_ple, w_ple_gate, w_ple_proj, g_final, y_buf)


def _layer(x, p, g_mix, w_in, w_pool_grp, pool_scale, w_pool_up, w_attn_up, w_out, g_ffn, w_router, b_router,
           w_gu, b_gu, w_down, b_down, g_ple, w_ple_gate, w_ple_proj, g_final, final_norm):
    b, s, d = x.shape
    n = b * s
    pool_w = len(POOL_WINDOWS) * POOL_CH
    att_w = N_HEADS * HEAD_DIM
    seq_tile = min(ROW_TILE, s)
    tok_tile = min(ROW_TILE, n)
    sort_tile = min(SORT_TILE, n)
    attn_blk = min(ATTN_BLOCK, s)

    row = lambda v: v.reshape(1, -1)
    w_in = w_in.astype(BF16)
    w_u, w_qkv = w_in[:, :pool_w], w_in[:, pool_w:pool_w + 3 * att_w]
    w_gp, w_ga = w_in[:, pool_w + 3 * att_w:pool_w + 3 * att_w + d], w_in[:, pool_w + 3 * att_w + d:]
    q, k, v, pool_branch, g_att = _mix_in(x, row(g_mix), w_u, w_qkv, w_gp, w_ga, w_pool_grp.astype(BF16),
                                          row(pool_scale), w_pool_up.astype(BF16), seq_tile)
    y_att = _attention(q, k, v, attn_blk)

    wr_hi = w_router.astype(BF16)
    wr_lo = (w_router - wr_hi.astype(F32)).astype(BF16)
    x1, h2, slots, w_top, tile_counts = _mix_out(
        x.reshape(n, d), y_att.reshape(n, att_w), pool_branch.reshape(n, d), g_att.reshape(n, d),
        w_attn_up.astype(BF16), w_out.astype(BF16), row(g_ffn), wr_hi, wr_lo, row(b_router), tok_tile, sort_tile)

    n_tiles = n // sort_tile
    seg_rows = -(-tile_counts[:, 0, :].astype(jnp.int32) // SEG_ALIGN) * SEG_ALIGN
    seg_src = jnp.cumsum(seg_rows, axis=1) - seg_rows
    group_rows = jnp.sum(seg_rows, axis=0)
    padded = (group_rows + EXPERT_BLOCK - 1) // EXPERT_BLOCK * EXPERT_BLOCK
    pad_ends = jnp.cumsum(padded)
    seg_dst = (pad_ends - padded)[None, :] + jnp.cumsum(seg_rows, axis=0) - seg_rows
    n_blocks = -(-(n * TOP_K + n_tiles * N_EXPERTS * (SEG_ALIGN - 1)) // EXPERT_BLOCK) + N_EXPERTS
    block_starts = jnp.arange(n_blocks, dtype=jnp.int32) * EXPERT_BLOCK
    block_e = jnp.minimum(jnp.sum((pad_ends[None, :] <= block_starts[:, None]).astype(jnp.int32), axis=1),
                          N_EXPERTS - 1)
    n_used = (pad_ends[-1:] // EXPERT_BLOCK).astype(jnp.int32)
    tile_rows = jnp.sum(seg_rows, axis=1)
    chunks = _chunk_table(seg_rows, seg_src, seg_dst, sort_tile)

    buf = _dispatch(h2, slots, pad_ends, group_rows, tile_rows, chunks, n_blocks * EXPERT_BLOCK, sort_tile)
    y_buf = _experts(buf, block_e, n_used, w_gu, b_gu[:, None, :], w_down, b_down[:, None, :])
    out = _combine(slots, w_top, x1, p.reshape(n, -1), y_buf, tile_rows, chunks, row(g_ple),
                   w_ple_gate.astype(BF16), w_ple_proj.astype(BF16), row(g_final), final_norm, sort_tile)
    return out.reshape(b, s, d)


def kernel(x, p, g_mix, w_in, w_pool_grp, pool_scale, w_pool_up, w_attn_up, w_out, g_ffn, w_router, b_router,
           w_gu, b_gu, w_down, b_down, g_ple, w_ple_gate, w_ple_proj, g_final):
    depth = p.shape[0]
    for i in range(depth):
        x = _layer(x, p[i], g_mix[i], w_in[i], w_pool_grp[i], pool_scale[i], w_pool_up[i], w_attn_up[i], w_out[i],
                   g_ffn[i], w_router[i], b_router[i], w_gu[i], b_gu[i], w_down[i], b_down[i], g_ple[i],
                   w_ple_gate[i], w_ple_proj[i], g_final, i == depth - 1)
    return x
```

```python
import functools

import jax
import jax.numpy as jnp
from jax import lax
from jax.experimental import pallas as pl
from jax.experimental.pallas import tpu as pltpu

F32 = jnp.float32
BF16 = jnp.bfloat16

N_HEADS = 8
HEAD_DIM = 64
POOL_WINDOWS = (2, 4, 8, 16)
POOL_CH = 128
POOL_HIST = 16
N_EXPERTS = 32
TOP_K = 4
EXPERT_BLOCK = 512
SEG_ALIGN = 8
BIG_CHUNK = 32
WAIT_CHUNK = 256
SMALL_CAP = N_EXPERTS * (BIG_CHUNK // SEG_ALIGN - 1)
SRC_BITS = 12
ISSUE_UNROLL = 4
SWIGLU_LIMIT = 7.0
SWIGLU_ALPHA = 1.702
EPS = 1e-6
LOG2_E = 1.4426950408889634
UNDERFLOW_BITS = 160.0
LANES = 128
ROW_TILE = 512
SORT_TILE = 256
ATTN_BLOCK = 256
VMEM_LIMIT = 56 * 1024 * 1024


def _params(n_axes, flags=None):
    return pltpu.CompilerParams(dimension_semantics=("arbitrary",) * n_axes, vmem_limit_bytes=VMEM_LIMIT,
                                flags=flags)


def _rms(xf, g):
    return xf * lax.rsqrt(jnp.mean(xf * xf, axis=-1, keepdims=True) + EPS) * g


def _dot(a, b):
    return jnp.dot(a, b, preferred_element_type=F32)


def _split_bf16(a):
    hi = a.astype(BF16)
    lo = (a - hi.astype(F32)).astype(BF16)
    return hi, lo


def _mix_in_kernel(x_ref, g_ref, wu_ref, wqkv_ref, wgp_ref, wga_ref, wgrp_ref, pscale_ref, wpu_ref,
                   q_ref, k_ref, v_ref, pb_ref, ga_ref, hist_ref):
    s = pl.program_id(1)
    ts = x_ref.shape[1]
    att_w = q_ref.shape[2]

    @pl.when(s == 0)
    def _():
        hist_ref[...] = jnp.zeros_like(hist_ref)

    h = _rms(x_ref[0], g_ref[...]).astype(BF16)
    u = _dot(h, wu_ref[...])
    ext = jnp.concatenate([hist_ref[...], u], axis=0)
    hist_ref[...] = u[ts - POOL_HIST:, :]
    t_pos = s * ts + lax.broadcasted_iota(jnp.int32, (ts, 1), 0)
    y_groups = []
    for g, w in enumerate(POOL_WINDOWS):
        a = ext[:, g * POOL_CH:(g + 1) * POOL_CH]
        sh = 1
        while sh < w:
            a = a + pltpu.roll(a, sh, axis=0)
            sh *= 2
        count = jnp.minimum(t_pos + 1, w).astype(F32)
        pooled = a[POOL_HIST:, :] / count - u[:, g * POOL_CH:(g + 1) * POOL_CH]
        y_groups.append(_dot(pooled.astype(BF16), wgrp_ref[g]))
    y_pool = jnp.concatenate(y_groups, axis=1) * pscale_ref[...]
    pool_up = _dot(y_pool.astype(BF16), wpu_ref[...])
    g_pool = jax.nn.sigmoid(_dot(h, wgp_ref[...]))
    pb_ref[0] = (g_pool * pool_up).astype(BF16)
    ga_ref[0] = jax.nn.sigmoid(_dot(h, wga_ref[...])).astype(BF16)
    qkv = _dot(h, wqkv_ref[...])
    q_ref[0] = (qkv[:, :att_w] * (HEAD_DIM ** -0.5 * LOG2_E)).astype(BF16)
    k_ref[0] = qkv[:, att_w:2 * att_w].astype(BF16)
    v_ref[0] = qkv[:, 2 * att_w:].astype(BF16)


def _mix_in(x, g_mix, w_u, w_qkv, w_gp, w_ga, w_grp, pool_scale, w_pool_up, ts):
    b, s, d = x.shape
    pool_w = w_u.shape[1]
    att_w = w_qkv.shape[1] // 3
    const = lambda *shape: pl.BlockSpec(shape, lambda i, j: (0,) * len(shape))
    tile = lambda w: pl.BlockSpec((1, ts, w), lambda i, j: (i, j, 0))
    return pl.pallas_call(
        _mix_in_kernel,
        grid=(b, s // ts),
        in_specs=[tile(d), const(1, d), const(d, pool_w), const(d, 3 * att_w), const(d, d), const(d, d),
                  const(len(POOL_WINDOWS), POOL_CH, POOL_CH), const(1, pool_w), const(pool_w, d)],
        out_specs=[tile(att_w), tile(att_w), tile(att_w), tile(d), tile(d)],
        out_shape=[jax.ShapeDtypeStruct((b, s, att_w), BF16)] * 3 + [jax.ShapeDtypeStruct((b, s, d), BF16)] * 2,
        scratch_shapes=[pltpu.VMEM((POOL_HIST, pool_w), F32)],
        compiler_params=_params(2),
        name="mix_in",
    )(x, g_mix, w_u, w_qkv, w_gp, w_ga, w_grp, pool_scale, w_pool_up)


def _attn_kernel(q_ref, k_ref, v_ref, tri_ref, o_ref, *, blk):
    i = pl.program_id(2)
    q2 = q_ref[0]
    lane = lax.broadcasted_iota(jnp.int32, (blk, LANES), 1)
    row = lax.broadcasted_iota(jnp.int32, (blk, blk), 0)
    col = lax.broadcasted_iota(jnp.int32, (blk, blk), 1)
    causal = col < row
    heads = (lane < HEAD_DIM, lane >= HEAD_DIM)
    qs = [jnp.where(hm, q2, jnp.zeros_like(q2)) for hm in heads]

    def keys_values(j):
        start = pl.multiple_of(j * blk, blk)
        return k_ref[0, pl.ds(start, blk), :], v_ref[0, pl.ds(start, blk), :]

    def scores(qh, kj, diagonal):
        z = lax.dot_general(qh, kj, (((1,), (1,)), ((), ())), preferred_element_type=F32)
        softplus = jnp.maximum(z, 0.0) + jnp.log2(1.0 + jnp.exp2(-jnp.abs(z)))
        if diagonal:
            softplus = jnp.where(causal, softplus, 0.0)
        return z, _dot(softplus.astype(BF16), tri_ref[...])

    def accumulate(z, incl, acc, run, vj, diagonal):
        w = jnp.exp2(z - incl - jnp.concatenate([run] * (blk // LANES), axis=1))
        if diagonal:
            w = jnp.where(causal, w, 0.0)
        return acc + _dot(w.astype(BF16), vj), run + jnp.broadcast_to(incl[:, 0:1], (blk, LANES))

    has_prev = i > 0
    kd, vd = keys_values(i)
    kp, vp = keys_values(jnp.maximum(i - 1, 0))
    front_d = [scores(qh, kd, True) for qh in qs]
    front_p = [scores(qh, kp, False) for qh in qs]
    zero = jnp.zeros((blk, LANES), F32)
    accs, runs = [], []
    for (zd, incl_d), (zp, incl_p) in zip(front_d, front_p):
        acc_d, run_d = accumulate(zd, incl_d, zero, zero, vd, True)
        acc_p, run_p = accumulate(zp, incl_p, acc_d, run_d, vp, False)
        accs.append(jnp.where(has_prev, acc_p, acc_d))
        runs.append(jnp.where(has_prev, run_p, run_d))

    def live(c):
        n, _, _, min_run = c
        return jnp.logical_and(n < i, min_run < UNDERFLOW_BITS)

    def step(c):
        n, accs, runs, _ = c
        kj, vj = keys_values(i - 1 - n)
        front = [scores(qh, kj, False) for qh in qs]
        back = [accumulate(z, incl, acc, run, vj, False) for (z, incl), acc, run in zip(front, accs, runs)]
        accs, runs = [a for a, _ in back], [r for _, r in back]
        return n + 1, accs, runs, jnp.minimum(jnp.min(runs[0]), jnp.min(runs[1]))

    min_run = jnp.minimum(jnp.min(runs[0]), jnp.min(runs[1]))
    _, accs, _, _ = lax.while_loop(live, step, (jnp.int32(1), accs, runs, min_run))
    o_ref[0] = jnp.where(heads[0], accs[0], accs[1]).astype(o_ref.dtype)


def _attention(q, k, v, blk):
    b, s, att_w = q.shape
    tri = (jnp.arange(blk)[:, None] >= jnp.arange(blk)[None, :]).astype(BF16)
    qo_spec = pl.BlockSpec((1, blk, LANES), lambda bi, hp, i: (bi, i, hp))
    kv_spec = pl.BlockSpec((1, s, LANES), lambda bi, hp, i: (bi, 0, hp))
    return pl.pallas_call(
        functools.partial(_attn_kernel, blk=blk),
        grid=(b, att_w // LANES, s // blk),
        in_specs=[qo_spec, kv_spec, kv_spec, pl.BlockSpec((blk, blk), lambda bi, hp, i: (0, 0))],
        out_specs=qo_spec,
        out_shape=jax.ShapeDtypeStruct((b, s, att_w), BF16),
        compiler_params=_params(3),
        name="attn",
    )(q, k, v, tri)


def _mix_out_kernel(x_ref, ya_ref, pb_ref, ga_ref, wau_ref, wo_ref, gf_ref, wrh_ref, wrl_ref, br_ref, lt_ref, up_ref,
                    x1_ref, h2_ref, s_ref, w_ref, cnt_ref):
    t = x_ref.shape[0]

    att_up = _dot(ya_ref[...], wau_ref[...])
    merged = pb_ref[...].astype(F32) + ga_ref[...].astype(F32) * att_up
    x1 = x_ref[...] + _dot(merged.astype(BF16), wo_ref[...])
    x1_ref[...] = x1
    h2 = _rms(x1, gf_ref[...])
    h2_ref[...] = h2.astype(h2_ref.dtype)

    hi, lo = _split_bf16(h2)
    logits = _dot(hi, wrh_ref[...]) + _dot(lo, wrh_ref[...]) + _dot(hi, wrl_ref[...]) + br_ref[...]
    lane_e = lax.broadcasted_iota(jnp.int32, (t, N_EXPERTS), 1)
    vals, idxs = [], []
    for _ in range(TOP_K):
        m = jnp.max(logits, axis=-1, keepdims=True)
        idx = jnp.min(jnp.where(logits == m, lane_e, N_EXPERTS), axis=-1, keepdims=True)
        vals.append(m)
        idxs.append(idx)
        logits = jnp.where(lane_e == idx, -jnp.inf, logits)
    exps = [jnp.exp(m - vals[0]) for m in vals]
    denom = exps[0] + exps[1] + exps[2] + exps[3]

    onehot = jnp.zeros((t, N_EXPERTS), F32)
    for idx in idxs:
        onehot = onehot + jnp.where(lane_e == idx, 1.0, 0.0)
    pos = _dot(lt_ref[...], onehot.astype(BF16))
    sort_tile = t // cnt_ref.shape[0]
    seg_starts = []
    for j in range(cnt_ref.shape[0]):
        cnt = jnp.sum(onehot[j * sort_tile:(j + 1) * sort_tile], axis=0, keepdims=True)
        cnt_ref[j] = cnt
        seg_rows = jnp.ceil(cnt * (1.0 / SEG_ALIGN)) * SEG_ALIGN
        before = _dot(jnp.broadcast_to(seg_rows, (SEG_ALIGN, N_EXPERTS)).astype(BF16), up_ref[...])[0:1]
        seg_starts.append(jnp.broadcast_to(before, (sort_tile, N_EXPERTS)))
    pos = pos + jnp.concatenate(seg_starts, axis=0)

    lane = lax.broadcasted_iota(jnp.int32, (t, LANES), 1)
    s_out = jnp.zeros((t, LANES), jnp.int32)
    w_out = jnp.zeros((t, LANES), F32)
    for kk in range(TOP_K):
        slot = jnp.sum(jnp.where(lane_e == idxs[kk], pos, 0.0), axis=-1, keepdims=True).astype(jnp.int32)
        s_out = jnp.where(lane == kk, slot, s_out)
        w_out = jnp.where(lane == kk, exps[kk] / denom, w_out)
    s_ref[...] = s_out
    w_ref[...] = w_out


def _mix_out(x2d, y_att, pool_branch, g_att, w_attn_up, w_out, g_ffn, wr_hi, wr_lo, b_router, t, sort_tile):
    n, d = x2d.shape
    att_w = y_att.shape[1]
    r, c = jnp.arange(t)[:, None], jnp.arange(t)[None, :]
    lower = jnp.logical_and(r > c, r // sort_tile == c // sort_tile).astype(BF16)
    experts = jnp.arange(N_EXPERTS)
    before = (experts[:, None] < experts[None, :]).astype(BF16)
    const = lambda *shape: pl.BlockSpec(shape, lambda i: (0,) * len(shape))
    tile = lambda w: pl.BlockSpec((t, w), lambda i: (i, 0))
    return pl.pallas_call(
        _mix_out_kernel,
        grid=(n // t,),
        in_specs=[tile(d), tile(att_w), tile(d), tile(d), const(att_w, d), const(d, d), const(1, d),
                  const(d, N_EXPERTS), const(d, N_EXPERTS), const(1, N_EXPERTS), const(t, t),
                  const(N_EXPERTS, N_EXPERTS)],
        out_specs=[tile(d), tile(d), tile(LANES), tile(LANES),
                   pl.BlockSpec((t // sort_tile, 1, N_EXPERTS), lambda i: (i, 0, 0))],
        out_shape=[jax.ShapeDtypeStruct((n, d), F32), jax.ShapeDtypeStruct((n, d), BF16),
                   jax.ShapeDtypeStruct((n, LANES), jnp.int32), jax.ShapeDtypeStruct((n, LANES), F32),
                   jax.ShapeDtypeStruct((n // sort_tile, 1, N_EXPERTS), F32)],
        compiler_params=_params(1),
        name="mix_out",
    )(x2d, y_att, pool_branch, g_att, w_attn_up, w_out, g_ffn, wr_hi, wr_lo, b_router, lower, before)


def _chunk_table(seg_rows, seg_src, seg_dst, sort_tile):
    n_big = seg_rows // BIG_CHUNK
    n_small = (seg_rows - n_big * BIG_CHUNK) // SEG_ALIGN

    def copies(counts, first_row, rows, capacity):
        ends = jnp.cumsum(counts, axis=1)
        starts = ends - counts
        idx = jnp.arange(capacity, dtype=jnp.int32)[None, :, None]
        mine = jnp.logical_and(starts[:, None, :] <= idx, idx < ends[:, None, :])
        pick = lambda a: jnp.sum(jnp.where(mine, a[:, None, :], 0), axis=2)
        row = pick(first_row) + (idx[:, :, 0] - pick(starts)) * rows
        return (pick(seg_src) + row) // SEG_ALIGN + ((pick(seg_dst) + row) // SEG_ALIGN << SRC_BITS)

    big_cap = (TOP_K * sort_tile + N_EXPERTS * (SEG_ALIGN - 1)) // BIG_CHUNK
    table = jnp.concatenate([jnp.sum(n_big, axis=1, keepdims=True), jnp.sum(n_small, axis=1, keepdims=True),
                             copies(n_big, jnp.zeros_like(n_big), BIG_CHUNK, big_cap),
                             copies(n_small, n_big * BIG_CHUNK, SEG_ALIGN, SMALL_CAP)], axis=1)
    return table[:, None, :].astype(jnp.int32)


def _for_tile_chunks(tab_ref, act):
    big_cap = tab_ref.shape[2] - 2 - SMALL_CAP

    def run(first, count, rows):
        def one(c):
            packed = tab_ref[0, 0, first + c]
            act(pl.multiple_of(jnp.bitwise_and(packed, (1 << SRC_BITS) - 1) * SEG_ALIGN, SEG_ALIGN),
                pl.multiple_of(lax.shift_right_logical(packed, SRC_BITS) * SEG_ALIGN, SEG_ALIGN), rows)

        def group(c, carry):
            for u in range(ISSUE_UNROLL):
                one(ISSUE_UNROLL * c + u)
            return carry

        groups = lax.shift_right_logical(count, ISSUE_UNROLL.bit_length() - 1)
        lax.fori_loop(0, groups, group, 0)
        lax.fori_loop(groups * ISSUE_UNROLL, count, lambda c, carry: (one(c), carry)[1], 0)

    run(2, tab_ref[0, 0, 0], BIG_CHUNK)
    run(2 + big_cap, tab_ref[0, 0, 1], SEG_ALIGN)


def _wait_rows(total, wait_chunk):
    n_big = lax.shift_right_logical(total, WAIT_CHUNK.bit_length() - 1)

    def big(j, c):
        wait_chunk(WAIT_CHUNK)
        return c

    lax.fori_loop(0, n_big, big, 0)
    rest = total - n_big * WAIT_CHUNK
    size = WAIT_CHUNK // 2
    while size >= SEG_ALIGN:
        @pl.when(jnp.bitwise_and(rest, size) != 0)
        def _(size=size):
            wait_chunk(size)
        size //= 2


def _dispatch_kernel(pe_ref, cnt_ref, tot_ref, tab_ref, slot_ref, h_ref, buf_ref, zero_ref, sorted_ref, sems, zsem):
    t = h_ref.shape[0]
    n_slots = sorted_ref.shape[1]
    n_blocks = buf_ref.shape[0] // EXPERT_BLOCK
    i = pl.program_id(0)
    cur = lax.rem(i, 2)

    def copy(into, in_tile, in_buf, rows):
        return pltpu.make_async_copy(sorted_ref.at[into, pl.ds(in_tile, rows)], buf_ref.at[pl.ds(in_buf, rows)],
                                     sems.at[into])

    def wait_tile(tile, into):
        _wait_rows(tot_ref[tile], lambda rows: copy(into, 0, 0, rows).wait())

    @pl.when(i == 0)
    def _():
        zero_ref[...] = jnp.zeros_like(zero_ref)

        def zero_copy(start):
            start = pl.multiple_of(start, EXPERT_BLOCK)
            return pltpu.make_async_copy(zero_ref, buf_ref.at[pl.ds(start, EXPERT_BLOCK)], zsem)

        def start_last(e, c):
            @pl.when(cnt_ref[e] > 0)
            def _():
                zero_copy(pe_ref[e] - EXPERT_BLOCK).start()
            return c

        def wait_last(e, c):
            @pl.when(cnt_ref[e] > 0)
            def _():
                zero_copy(pe_ref[e] - EXPERT_BLOCK).wait()
            return c

        def start_tail(b, c):
            zero_copy(b * EXPERT_BLOCK).start()
            return c

        def wait_tail(b, c):
            zero_copy(b * EXPERT_BLOCK).wait()
            return c

        n_used = pe_ref[N_EXPERTS - 1] // EXPERT_BLOCK
        lax.fori_loop(0, N_EXPERTS, start_last, 0)
        lax.fori_loop(n_used, n_blocks, start_tail, 0)
        lax.fori_loop(0, N_EXPERTS, wait_last, 0)
        lax.fori_loop(n_used, n_blocks, wait_tail, 0)

    @pl.when(i >= 2)
    def _():
        wait_tile(i - 2, cur)

    slots = slot_ref[...]
    slot_id = lax.broadcasted_iota(jnp.int32, (t, n_slots), 1)
    select = jnp.zeros((t, n_slots), F32)
    for kk in range(TOP_K):
        select = jnp.where(slot_id == slots[:, kk:kk + 1], 1.0, select)
    select = select.astype(BF16)
    sorted_ref[cur] = lax.dot_general(select, h_ref[...], (((0,), (0,)), ((), ())), preferred_element_type=F32)
    _for_tile_chunks(tab_ref, lambda in_tile, in_buf, rows: copy(cur, in_tile, in_buf, rows).start())

    @pl.when(i == pl.num_programs(0) - 1)
    def _():
        @pl.when(i >= 1)
        def _():
            wait_tile(i - 1, 1 - cur)
        wait_tile(i, cur)


def _dispatch(h2, slots, pad_ends, counts, tile_rows, chunks, n_pad, t):
    n, d = h2.shape
    n_slots = TOP_K * t + N_EXPERTS * SEG_ALIGN
    assert n_slots // SEG_ALIGN <= 1 << SRC_BITS and n_pad // SEG_ALIGN < 1 << (31 - SRC_BITS)
    return pl.pallas_call(
        _dispatch_kernel,
        grid_spec=pltpu.PrefetchScalarGridSpec(
            num_scalar_prefetch=3,
            grid=(n // t,),
            in_specs=[pl.BlockSpec((1, 1, chunks.shape[2]), lambda i, *_: (i, 0, 0), memory_space=pltpu.SMEM),
                      pl.BlockSpec((t, LANES), lambda i, *_: (i, 0)), pl.BlockSpec((t, d), lambda i, *_: (i, 0))],
            out_specs=pl.BlockSpec(memory_space=pl.ANY),
            scratch_shapes=[pltpu.VMEM((EXPERT_BLOCK, d), F32), pltpu.VMEM((2, n_slots, d), F32),
                            pltpu.SemaphoreType.DMA((2,)), pltpu.SemaphoreType.DMA],
        ),
        out_shape=jax.ShapeDtypeStruct((n_pad, d), F32),
        compiler_params=_params(1),
        name="dispatch",
    )(pad_ends, counts, tile_rows, chunks, slots, h2)


def _expert_kernel(be_ref, nu_ref, x_ref, wgu_ref, bgu_ref, wd_ref, bd_ref, y_ref, wgu_bf, wd_bf):
    d_exp = wd_ref.shape[1]
    i = pl.program_id(0)

    @pl.when(jnp.logical_or(i == 0, be_ref[i] != be_ref[jnp.maximum(i - 1, 0)]))
    def _():
        wgu_bf[...] = wgu_ref[0].astype(BF16)
        wd_bf[...] = wd_ref[0].astype(BF16)

    @pl.when(i < nu_ref[0])
    def _():
        gu = _dot(x_ref[...].astype(BF16), wgu_bf[...]) + bgu_ref[0]
        gate = jnp.minimum(gu[:, :d_exp], SWIGLU_LIMIT)
        up = jnp.clip(gu[:, d_exp:], -SWIGLU_LIMIT, SWIGLU_LIMIT)
        glu = gate * jax.nn.sigmoid(SWIGLU_ALPHA * gate)
        y_ref[...] = _dot(((up + 1.0) * glu).astype(BF16), wd_bf[...]) + bd_ref[0]

    @pl.when(i >= nu_ref[0])
    def _():
        y_ref[...] = jnp.zeros_like(y_ref)


def _experts(buf, block_e, n_used, w_gu, b_gu, w_down, b_down):
    n_pad, d = buf.shape
    d_exp = w_down.shape[1]
    used = lambda i, be, nu: jnp.minimum(i, nu[0] - 1)
    rows = pl.BlockSpec((EXPERT_BLOCK, d), lambda i, be, nu: (used(i, be, nu), 0))
    out_rows = pl.BlockSpec((EXPERT_BLOCK, d), lambda i, be, nu: (i, 0))
    per_e = lambda *shape: pl.BlockSpec((1,) + shape, lambda i, be, nu: (be[used(i, be, nu)], 0, 0))
    return pl.pallas_call(
        _expert_kernel,
        grid_spec=pltpu.PrefetchScalarGridSpec(
            num_scalar_prefetch=2,
            grid=(n_pad // EXPERT_BLOCK,),
            in_specs=[rows, per_e(d, 2 * d_exp), per_e(1, 2 * d_exp), per_e(d_exp, d), per_e(1, d)],
            out_specs=out_rows,
            scratch_shapes=[pltpu.VMEM((d, 2 * d_exp), BF16), pltpu.VMEM((d_exp, d), BF16)],
        ),
        out_shape=jax.ShapeDtypeStruct((n_pad, d), F32),
        compiler_params=_params(1),
        name="experts",
    )(block_e, n_used, buf, w_gu, b_gu, w_down, b_down)


def _combine_kernel(tot_ref, tab_ref, next_tab_ref, slot_ref, tw_ref, x1_ref, p_ref, gp_ref, wg_ref, wp_ref, gfin_ref,
                    y_ref, o_ref, rows_ref, sems, *, final_norm):
    t = x1_ref.shape[0]
    n_slots = rows_ref.shape[1]
    i = pl.program_id(0)
    cur = lax.rem(i, 2)

    def copy(into, in_tile, in_buf, rows):
        return pltpu.make_async_copy(y_ref.at[pl.ds(in_buf, rows)], rows_ref.at[into, pl.ds(in_tile, rows)],
                                     sems.at[into])

    def start_tile(table, into):
        _for_tile_chunks(table, lambda in_tile, in_buf, rows: copy(into, in_tile, in_buf, rows).start())

    @pl.when(i == 0)
    def _():
        rows_ref[...] = jnp.zeros_like(rows_ref)
        start_tile(tab_ref, 0)

    @pl.when(i + 1 < pl.num_programs(0))
    def _():
        start_tile(next_tab_ref, 1 - cur)

    _wait_rows(tot_ref[i], lambda rows: copy(cur, 0, 0, rows).wait())

    slots = slot_ref[...]
    tw = tw_ref[...]
    slot_id = lax.broadcasted_iota(jnp.int32, (t, n_slots), 1)
    weights = jnp.zeros((t, n_slots), F32)
    for kk in range(TOP_K):
        weights = jnp.where(slot_id == slots[:, kk:kk + 1], tw[:, kk:kk + 1], weights)
    w_hi, w_lo = _split_bf16(weights)
    y = rows_ref[cur].astype(BF16)
    x2 = x1_ref[...] + (_dot(w_hi, y) + _dot(w_lo, y))
    hp = _rms(x2, gp_ref[...]).astype(BF16)
    gate = jax.nn.sigmoid(_dot(hp, wg_ref[...]))
    x3 = x2 + gate * _dot(p_ref[...].astype(BF16), wp_ref[...])
    o_ref[...] = _rms(x3, gfin_ref[...]) if final_norm else x3


def _combine(slots, top_w, x1, p2d, y_buf, tile_rows, chunks, g_ple, w_ple_gate, w_ple_proj, g_final, final_norm, t):
    n, d = x1.shape
    ple = p2d.shape[1]
    n_slots = TOP_K * t + N_EXPERTS * SEG_ALIGN
    last = n // t - 1
    const = lambda *shape: pl.BlockSpec(shape, lambda i, *_: (0,) * len(shape))
    tile = lambda w: pl.BlockSpec((t, w), lambda i, *_: (i, 0))
    table = lambda step: pl.BlockSpec((1, 1, chunks.shape[2]), lambda i, *_: (step(i), 0, 0), memory_space=pltpu.SMEM)
    return pl.pallas_call(
        functools.partial(_combine_kernel, final_norm=final_norm),
        grid_spec=pltpu.PrefetchScalarGridSpec(
            num_scalar_prefetch=1,
            grid=(n // t,),
            in_specs=[table(lambda i: i), table(lambda i: jnp.minimum(i + 1, last)),
                      tile(LANES), tile(LANES), tile(d), tile(ple), const(1, d), const(d, d), const(ple, d),
                      const(1, d), pl.BlockSpec(memory_space=pl.ANY)],
            out_specs=tile(d),
            scratch_shapes=[pltpu.VMEM((2, n_slots, d), F32), pltpu.SemaphoreType.DMA((2,))],
        ),
        out_shape=jax.ShapeDtypeStruct((n, d), F32),
        compiler_params=_params(1),
        name="combine",
    )(tile_rows, chunks, chunks, slots, top_w, x1, p2d, g_ple, w_ple_gate, w_ple_proj, g_final, y_buf)


def _layer(x, p, g_mix, w_in, w_pool_grp, pool_scale, w_pool_up, w_attn_up, w_out, g_ffn, w_router, b_router,
           w_gu, b_gu, w_down, b_down, g_ple, w_ple_gate, w_ple_proj, g_final, final_norm):
    b, s, d = x.shape
    n = b * s
    pool_w = len(POOL_WINDOWS) * POOL_CH
    att_w = N_HEADS * HEAD_DIM
    seq_tile = min(ROW_TILE, s)
    tok_tile = min(ROW_TILE, n)
    sort_tile = min(SORT_TILE, n)
    attn_blk = min(ATTN_BLOCK, s)

    row = lambda v: v.reshape(1, -1)
    w_in = w_in.astype(BF16)
    w_u, w_qkv = w_in[:, :pool_w], w_in[:, pool_w:pool_w + 3 * att_w]
    w_gp, w_ga = w_in[:, pool_w + 3 * att_w:pool_w + 3 * att_w + d], w_in[:, pool_w + 3 * att_w + d:]
    q, k, v, pool_branch, g_att = _mix_in(x, row(g_mix), w_u, w_qkv, w_gp, w_ga, w_pool_grp.astype(BF16),
                                          row(pool_scale), w_pool_up.astype(BF16), seq_tile)
    y_att = _attention(q, k, v, attn_blk)

    wr_hi = w_router.astype(BF16)
    wr_lo = (w_router - wr_hi.astype(F32)).astype(BF16)
    x1, h2, slots, w_top, tile_counts = _mix_out(
        x.reshape(n, d), y_att.reshape(n, att_w), pool_branch.reshape(n, d), g_att.reshape(n, d),
        w_attn_up.astype(BF16), w_out.astype(BF16), row(g_ffn), wr_hi, wr_lo, row(b_router), tok_tile, sort_tile)

    n_tiles = n // sort_tile
    seg_rows = -(-tile_counts[:, 0, :].astype(jnp.int32) // SEG_ALIGN) * SEG_ALIGN
    seg_src = jnp.cumsum(seg_rows, axis=1) - seg_rows
    group_rows = jnp.sum(seg_rows, axis=0)
    padded = (group_rows + EXPERT_BLOCK - 1) // EXPERT_BLOCK * EXPERT_BLOCK
    pad_ends = jnp.cumsum(padded)
    seg_dst = (pad_ends - padded)[None, :] + jnp.cumsum(seg_rows, axis=0) - seg_rows
    n_blocks = -(-(n * TOP_K + n_tiles * N_EXPERTS * (SEG_ALIGN - 1)) // EXPERT_BLOCK) + N_EXPERTS
    block_starts = jnp.arange(n_blocks, dtype=jnp.int32) * EXPERT_BLOCK
    block_e = jnp.minimum(jnp.sum((pad_ends[None, :] <= block_starts[:, None]).astype(jnp.int32), axis=1),
                          N_EXPERTS - 1)
    n_used = (pad_ends[-1:] // EXPERT_BLOCK).astype(jnp.int32)
    tile_rows = jnp.sum(seg_rows, axis=1)
    chunks = _chunk_table(seg_rows, seg_src, seg_dst, sort_tile)

    buf = _dispatch(h2, slots, pad_ends, group_rows, tile_rows, chunks, n_blocks * EXPERT_BLOCK, sort_tile)
    y_buf = _experts(buf, block_e, n_used, w_gu, b_gu[:, None, :], w_down, b_down[:, None, :])
    out = _combine(slots, w_top, x1, p.reshape(n, -1), y_buf, tile_rows, chunks, row(g_ple),
                   w_ple_gate.astype(BF16), w_ple_proj.astype(BF16), row(g_final), final_norm, sort_tile)
    return out.reshape(b, s, d)


def kernel(x, p, g_mix, w_in, w_pool_grp, pool_scale, w_pool_up, w_attn_up, w_out, g_ffn, w_router, b_router,
           w_gu, b_gu, w_down, b_down, g_ple, w_ple_gate, w_ple_proj, g_final):
    depth = p.shape[0]
    for i in range(depth):
        x = _layer(x, p[i], g_mix[i], w_in[i], w_pool_grp[i], pool_scale[i], w_pool_up[i], w_attn_up[i], w_out[i],
                   g_ffn[i], w_router[i], b_router[i], w_gu[i], b_gu[i], w_down[i], b_down[i], g_ple[i],
                   w_ple_gate[i], w_ple_proj[i], g_final, i == depth - 1)
    return x
```
